```python
import jax, jax.numpy as jnp
from jax import lax
import numpy as np

D_MODEL = 2048
BATCH = 4
SEQ = 4096
DEPTH = 2

N_META = 16
CHUNK = 64
EPS = 1e-6

A_HEAD_DIM = 64
A_WIDTH = D_MODEL // 4
A_HEADS = A_WIDTH // A_HEAD_DIM
A_DECAY_RANK = 32
A_ICL_RANK = 32
A_GATE_RANK = 96
A_GN_EPS = 64e-5
B_DV = 128
B_WIDTH = D_MODEL // 4
B_HEADS = B_WIDTH // B_DV
B_DK = B_DV // 2
B_QK = B_HEADS * B_DK
B_GATE_RANK = 16
B_GATE_NORMALIZER = 16.0
C_DV = 256
C_WIDTH = D_MODEL // 2
C_HEADS = C_WIDTH // C_DV
C_DQK = C_DV // 2
C_QK = C_HEADS * C_DQK
C_CONV = 4
C_GATE_CAP = 15.0

MIX_WIDTH = A_WIDTH + B_WIDTH + C_WIDTH
A_COLS = 3 * A_WIDTH + A_DECAY_RANK + A_ICL_RANK + A_GATE_RANK
B_COLS = 2 * B_QK + B_WIDTH + B_GATE_RANK + B_WIDTH
C_COLS = 2 * C_QK + C_WIDTH + 2 * C_HEADS + C_WIDTH
N_IN = A_COLS + B_COLS + C_COLS
D_FF = ((8 * D_MODEL // 3 + 255) // 256) * 256

kernel_name = "hymba_rwkv7_gla_mlstm_trunk"


def _split(a, sizes):
    offs = [int(o) for o in np.cumsum(sizes)[:-1]]
    return jnp.split(a, offs, axis=-1)


def rmsnorm(x, g):
    xf = x.astype(jnp.float32)
    y = xf * lax.rsqrt(jnp.mean(xf * xf, axis=-1, keepdims=True) + EPS)
    return (y * g.astype(jnp.float32)).astype(x.dtype)


def _heads(a, n_heads):
    b, l = a.shape[:2]
    return a.reshape(b, l, n_heads, -1).transpose(0, 2, 1, 3)


def _unheads(a):
    b, h, l, d = a.shape
    return a.transpose(0, 2, 1, 3).reshape(b, l, h * d)


def _softcap(z, cap):
    return cap * jnp.tanh(z / cap)


def _token_shift(p):
    return jnp.pad(p, ((0, 0), (1, 0), (0, 0)))[:, :-1]


def _causal_conv(x, w, b):
    k_w = w.shape[0]
    l = x.shape[1]
    xp = jnp.pad(x, ((0, 0), (k_w - 1, 0), (0, 0)))
    y = b
    for j in range(k_w):
        y = y + xp[:, j:j + l] * w[j]
    return y


def _to_chunks(a):
    b, h, t = a.shape[:3]
    a = a.reshape((b, h, t // CHUNK, CHUNK) + a.shape[3:])
    return jnp.moveaxis(a, 2, 0)


def _from_chunks(y):
    nc, b, h, c, e = y.shape
    return jnp.moveaxis(y, 0, 2).reshape(b, h, nc * c, e)


def _run_chunked(step, state0, xs):
    meta = tuple(a[:, :, :N_META] for a in xs)
    real = tuple(_to_chunks(a[:, :, N_META:]) for a in xs)
    state, y_meta = step(state0, meta)
    _, y_real = lax.scan(step, state, real)
    return jnp.concatenate([y_meta, _from_chunks(y_real)], axis=2)


def rwkv7_mixer(p, mu, w0, w2, a0, a2, g2, k_k, k_a, r_k, ln_w, ln_b):
    dt = p.dtype
    p = p.astype(jnp.float32)
    bsz, l = p.shape[:2]
    p = p + mu * (_token_shift(p) - p)
    r, k, v, xw, xa, xg = _split(p, [A_WIDTH, A_WIDTH, A_WIDTH, A_DECAY_RANK, A_ICL_RANK, A_GATE_RANK])
    w_log = -jax.nn.softplus(-(w0 + jnp.tanh(xw) @ w2)) - 0.5
    decay = jnp.exp(-jnp.exp(w_log))
    a = jax.nn.sigmoid(a0 + xa @ a2)
    g = jax.nn.sigmoid(xg) @ g2
    hs = lambda t: t.reshape(bsz, l, A_HEADS, A_HEAD_DIM)
    kk = hs(k * k_k)
    kk = kk / jnp.maximum(jnp.linalg.norm(kk, axis=-1, keepdims=True), 1e-12)
    k = k * (1.0 + (a - 1.0) * k_a)
    r4, k4, v4, w4, a4 = hs(r), hs(k), hs(v), hs(decay), hs(a)
    bb = kk * a4
    tm = lambda t: jnp.moveaxis(t, 1, 0)

    def step(S, inp):
        r_t, w_t, k_t, v_t, kk_t, b_t = inp
        sa = jnp.einsum('bhvk,bhk->bhv', S, -kk_t)
        S = S * w_t[:, :, None, :] + sa[..., None] * b_t[:, :, None, :] + v_t[..., None] * k_t[:, :, None, :]
        return S, jnp.einsum('bhvk,bhk->bhv', S, r_t)

    S0 = jnp.zeros((bsz, A_HEADS, A_HEAD_DIM, A_HEAD_DIM), jnp.float32)
    _, y = lax.scan(step, S0, (tm(r4), tm(w4), tm(k4), tm(v4), tm(kk), tm(bb)))
    y = jnp.moveaxis(y, 0, 1)
    mean = jnp.mean(y, axis=-1, keepdims=True)
    var = jnp.mean(jnp.square(y - mean), axis=-1, keepdims=True)
    y = (y - mean) * lax.rsqrt(var + A_GN_EPS)
    y = y * ln_w.reshape(A_HEADS, A_HEAD_DIM) + ln_b.reshape(A_HEADS, A_HEAD_DIM)
    bonus = jnp.sum(r4 * k4 * r_k, axis=-1, keepdims=True) * v4
    out = (y + bonus).reshape(bsz, l, A_WIDTH) * g
    return out.astype(dt)


def _gla_step(S, inp):
    q, k, v, lg = inp
    c = q.shape[2]
    b = jnp.cumsum(lg, axis=2)
    b_last = b[:, :, -1:]
    o_inter = jnp.einsum('bhcd,bhde->bhce', q * jnp.exp(b), S)
    mask = jnp.tril(jnp.ones((c, c), bool))[:, :, None]
    dec = jnp.exp(jnp.where(mask, b[:, :, :, None, :] - b[:, :, None, :, :], -jnp.inf))
    scores = jnp.einsum('bhid,bhjd,bhijd->bhij', q, k, dec)
    o = o_inter + jnp.einsum('bhij,bhje->bhie', scores, v)
    S = jnp.exp(b_last[:, :, 0])[..., None] * S + jnp.einsum('bhcd,bhce->bhde', k * jnp.exp(b_last - b), v)
    return S, o


def gla_mixer(p, a2, ab, norm_w):
    dt = p.dtype
    p = p.astype(jnp.float32)
    bsz = p.shape[0]
    q, k, v, xa, g = _split(p, [B_QK, B_QK, B_WIDTH, B_GATE_RANK, B_WIDTH])
    q = _heads(q, B_HEADS) * (B_DK ** -0.5)
    k = _heads(k, B_HEADS)
    v = _heads(v, B_HEADS)
    lg = _heads(jax.nn.log_sigmoid(xa @ a2 + ab) / B_GATE_NORMALIZER, B_HEADS)
    S0 = jnp.zeros((bsz, B_HEADS, B_DK, B_DV), jnp.float32)
    o = _run_chunked(_gla_step, S0, (q, k, v, lg))
    o = o * lax.rsqrt(jnp.mean(o * o, axis=-1, keepdims=True) + EPS) * norm_w
    out = _unheads(o) * jax.nn.silu(g)
    return out.astype(dt)


def _mlstm_step(state, inp):
    Cm, n, m = state
    q, k, v, ig, lf = inp
    c = q.shape[2]
    b = jnp.cumsum(lf, axis=-1)
    mask = jnp.tril(jnp.ones((c, c), bool))
    log_w = jnp.where(mask, b[..., :, None] - b[..., None, :] + ig[..., None, :], -jnp.inf)
    log_prev = b + m[..., None]
    m_t = jnp.maximum(log_prev, jnp.max(log_w, axis=-1))
    w = jnp.exp(log_w - m_t[..., None])
    w_prev = jnp.exp(log_prev - m_t)
    s = jnp.einsum('bhtd,bhjd->bhtj', q, k) * w
    num = w_prev[..., None] * jnp.einsum('bhtd,bhde->bhte', q, Cm) + jnp.einsum('bhtj,bhje->bhte', s, v)
    den = w_prev * jnp.einsum('bhtd,bhd->bht', q, n) + jnp.sum(s, axis=-1)
    h = num / jnp.maximum(jnp.abs(den), jnp.exp(-m_t))[..., None]
    m_new = m_t[..., -1]
    w_end = jnp.exp(b[..., -1:] - b + ig - m_new[..., None])
    f_end = jnp.exp(b[..., -1] + m - m_new)
    Cm = f_end[..., None, None] * Cm + jnp.einsum('bhc,bhcd,bhce->bhde', w_end, k, v)
    n = f_end[..., None] * n + jnp.einsum('bhc,bhcd->bhd', w_end, k)
    return (Cm, n, m_new), h


def mlstm_mixer(p, conv_w, conv_b, ib, fb, norm_w):
    dt = p.dtype
    p = p.astype(jnp.float32)
    bsz = p.shape[0]
    q, k, v, ig, fg, o = _split(p, [C_QK, C_QK, C_WIDTH, C_HEADS, C_HEADS, C_WIDTH])
    qk = jax.nn.silu(_causal_conv(jnp.concatenate([q, k], axis=-1), conv_w, conv_b))
    q, k = _split(qk, [C_QK, C_QK])
    q = _heads(q, C_HEADS)
    k = _heads(k, C_HEADS) * (C_DQK ** -0.5)
    v = _heads(v, C_HEADS)
    ig = jnp.swapaxes(_softcap(ig + ib, C_GATE_CAP), 1, 2)
    lf = jnp.swapaxes(jax.nn.log_sigmoid(_softcap(fg + fb, C_GATE_CAP)), 1, 2)
    state0 = (jnp.zeros((bsz, C_HEADS, C_DQK, C_DV), jnp.float32),
              jnp.zeros((bsz, C_HEADS, C_DQK), jnp.float32),
              jnp.zeros((bsz, C_HEADS), jnp.float32))
    h = _run_chunked(_mlstm_step, state0, (q, k, v, ig, lf))
    h = h * lax.rsqrt(jnp.mean(h * h, axis=-1, keepdims=True) + EPS)
    out = _unheads(h) * norm_w * jax.nn.sigmoid(o)
    return out.astype(dt)


def setup_inputs(seed: int = 0) -> dict:
    key = jax.random.key(seed)
    ks = iter(jax.random.split(key, 40))
    nrm = lambda shape, s: jax.random.normal(next(ks), shape, jnp.float32) * s
    uni = lambda shape, lo, hi: jax.random.uniform(next(ks), shape, jnp.float32, lo, hi)
    gain = lambda shape: 1.0 + nrm(shape, 0.05)
    L = DEPTH
    return {
        "x": nrm((BATCH, SEQ, D_MODEL), 1.0),
        "meta_tokens": nrm((N_META, D_MODEL), 1.0),
        "norm_mix": gain((L, D_MODEL)),
        "w_in": nrm((L, D_MODEL, N_IN), D_MODEL ** -0.5),
        "rw_mu": uni((L, A_COLS), 0.0, 1.0),
        "rw_w0": uni((L, A_WIDTH), -5.0, 0.0),
        "rw_w2": nrm((L, A_DECAY_RANK, A_WIDTH), 0.1),
        "rw_a0": nrm((L, A_WIDTH), 0.1),
        "rw_a2": nrm((L, A_ICL_RANK, A_WIDTH), 0.5 * A_ICL_RANK ** -0.5),
        "rw_g2": nrm((L, A_GATE_RANK, A_WIDTH), A_GATE_RANK ** -0.5),
        "rw_kk": 0.85 + nrm((L, A_WIDTH), 0.05),
        "rw_ka": 1.0 + nrm((L, A_WIDTH), 0.05),
        "rw_rk": nrm((L, A_HEADS, A_HEAD_DIM), 0.1),
        "rw_ln_w": gain((L, A_WIDTH)),
        "rw_ln_b": nrm((L, A_WIDTH), 0.02),
        "gla_a2": nrm((L, B_GATE_RANK, B_QK), B_GATE_RANK ** -0.5),
        "gla_ab": nrm((L, B_QK), 0.1),
        "gla_norm": gain((L, B_DV)),
        "ml_conv_w": nrm((L, C_CONV, 2 * C_QK), C_CONV ** -0.5),
        "ml_conv_b": nrm((L, 2 * C_QK), 0.02),
        "ml_ib": nrm((L, C_HEADS), 0.1),
        "ml_fb": uni((L, C_HEADS), 3.0, 6.0),
        "ml_norm": gain((L, C_WIDTH)),
        "w_out": nrm((L, MIX_WIDTH, D_MODEL), MIX_WIDTH ** -0.5),
        "norm_ffn": gain((L, D_MODEL)),
        "ffn_w1": nrm((L, D_MODEL, D_FF), D_MODEL ** -0.5),
        "ffn_w3": nrm((L, D_MODEL, D_FF), D_MODEL ** -0.5),
        "ffn_w2": nrm((L, D_FF, D_MODEL), D_FF ** -0.5),
        "norm_final": gain((D_MODEL,)),
    }


def reference(x, meta_tokens, norm_mix, w_in, rw_mu, rw_w0, rw_w2, rw_a0, rw_a2, rw_g2,
              rw_kk, rw_ka, rw_rk, rw_ln_w, rw_ln_b, gla_a2, gla_ab, gla_norm,
              ml_conv_w, ml_conv_b, ml_ib, ml_fb, ml_norm, w_out, norm_ffn,
              ffn_w1, ffn_w3, ffn_w2, norm_final):
    bsz = x.shape[0]
    meta = jnp.broadcast_to(meta_tokens[None].astype(x.dtype), (bsz, N_META, D_MODEL))
    h = jnp.concatenate([meta, x], axis=1)
    for l in range(DEPTH):
        u = rmsnorm(h, norm_mix[l])
        p = u @ w_in[l]
        p_a, p_b, p_c = _split(p, [A_COLS, B_COLS, C_COLS])
        y_a = rwkv7_mixer(p_a, rw_mu[l], rw_w0[l], rw_w2[l], rw_a0[l], rw_a2[l], rw_g2[l],
                          rw_kk[l], rw_ka[l], rw_rk[l], rw_ln_w[l], rw_ln_b[l])
        y_b = gla_mixer(p_b, gla_a2[l], gla_ab[l], gla_norm[l])
        y_c = mlstm_mixer(p_c, ml_conv_w[l], ml_conv_b[l], ml_ib[l], ml_fb[l], ml_norm[l])
        h = h + jnp.concatenate([y_a, y_b, y_c], axis=-1) @ w_out[l]
        u = rmsnorm(h, norm_ffn[l])
        h = h + (jax.nn.silu(u @ ffn_w1[l]) * (u @ ffn_w3[l])) @ ffn_w2[l]
    return rmsnorm(h, norm_final)[:, N_META:]
```

```python
import functools

import jax
import jax.numpy as jnp
from jax import lax
from jax.experimental import pallas as pl
from jax.experimental.pallas import tpu as pltpu

F32 = jnp.float32
BF16 = jnp.bfloat16
HI = lax.Precision.HIGHEST

EPS = 1e-6
N_META = 16
CHUNK = 64

A_HEAD_DIM = 64
A_GN_EPS = 64e-5
B_DV = 128
B_GATE_NORMALIZER = 16.0
B_SUB = 16
C_DV = 256
C_CONV = 4
C_GATE_CAP = 15.0

LANE = 128
VMEM_LIMIT = 56 * 1024 * 1024

NT = (((1,), (1,)), ((), ()))
TN = (((0,), (0,)), ((), ()))


def _rup(n, m):
    return -(-n // m) * m


def _dot(a, b, precision=HI):
    return jnp.dot(a, b, precision=precision, preferred_element_type=F32)


def _dot_nt(a, b, precision=HI):
    return lax.dot_general(a, b, NT, precision=precision, preferred_element_type=F32)


def _dot_tn(a, b, precision=HI):
    return lax.dot_general(a, b, TN, precision=precision, preferred_element_type=F32)


def _softplus(z):
    return jnp.maximum(z, 0.0) + jnp.log1p(jnp.exp(-jnp.abs(z)))


def _log_sigmoid(z):
    return -_softplus(-z)


def _tril_mask(n, strict=False):
    r = lax.broadcasted_iota(jnp.int32, (n, n), 0)
    c = lax.broadcasted_iota(jnp.int32, (n, n), 1)
    return (r > c) if strict else (r >= c)


def _params(n_axes):
    return pltpu.CompilerParams(
        dimension_semantics=("arbitrary",) * n_axes, vmem_limit_bytes=VMEM_LIMIT)


def _norm_matmul_kernel(h_ref, g_ref, w_ref, o_ref, u_ref):
    @pl.when(pl.program_id(1) == 0)
    def _():
        x = h_ref[...]
        ms = jnp.mean(x * x, axis=-1, keepdims=True)
        u_ref[...] = (x * lax.rsqrt(ms + EPS) * g_ref[...]).astype(BF16)

    o_ref[...] = jnp.dot(u_ref[...], w_ref[...], preferred_element_type=F32)


def _norm_matmul(h, g, w, tm, tn):
    m, d = h.shape
    n = w.shape[1]
    return pl.pallas_call(
        _norm_matmul_kernel,
        grid=(m // tm, n // tn),
        in_specs=[
            pl.BlockSpec((tm, d), lambda i, j: (i, 0)),
            pl.BlockSpec((1, d), lambda i, j: (0, 0)),
            pl.BlockSpec((d, tn), lambda i, j: (0, j)),
        ],
        out_specs=pl.BlockSpec((tm, tn), lambda i, j: (i, j)),
        out_shape=jax.ShapeDtypeStruct((m, n), F32),
        scratch_shapes=[pltpu.VMEM((tm, d), BF16)],
        compiler_params=_params(2),
    )(h, g, w)


def _rwkv_kernel(p_ref, mu_ref, w0_ref, w2_ref, a0_ref, a2_ref, g2_ref, kk_ref, ka_ref,
                 rk_ref, lnw_ref, lnb_ref, o_ref, s_ref, prev_ref, *, width, d_rank,
                 a_rank, g_rank):
    c = CHUNK
    n = A_HEAD_DIM
    heads = width // n

    @pl.when(pl.program_id(1) == 0)
    def _():
        s_ref[...] = jnp.zeros_like(s_ref)
        prev_ref[...] = jnp.zeros_like(prev_ref)

    p = p_ref[...]
    row = lax.broadcasted_iota(jnp.int32, p.shape, 0)
    shifted = jnp.where(row == 0, prev_ref[...], pltpu.roll(p, 1, axis=0))
    prev_ref[...] = p[c - 1:c, :]
    x = p + mu_ref[...] * (shifted - p)

    off_w = 3 * width
    off_a = off_w + _rup(d_rank, LANE)
    off_g = off_a + _rup(a_rank, LANE)
    r = x[:, 0:width]
    k = x[:, width:2 * width]
    v = x[:, 2 * width:3 * width]
    xw = x[:, off_w:off_w + d_rank]
    xa = x[:, off_a:off_a + a_rank]
    xg = x[:, off_g:off_g + g_rank]

    w_log = -_softplus(-(w0_ref[...] + _dot(jnp.tanh(xw), w2_ref[...]))) - 0.5
    ld = -jnp.exp(w_log)
    a = jax.nn.sigmoid(a0_ref[...] + _dot(xa, a2_ref[...]))
    g = _dot(jax.nn.sigmoid(xg), g2_ref[...])
    kk = k * kk_ref[...]
    k2 = k * (1.0 + (a - 1.0) * ka_ref[...])

    tri = _tril_mask(c).astype(F32)
    cum = _dot(tri, ld)
    cum_prev = cum - ld
    e_pos = jnp.exp(cum)
    e_neg = jnp.exp(-cum)
    e_prev = jnp.exp(cum_prev)
    e_end = jnp.exp(cum[c - 1:c, :] - cum)

    low = _tril_mask(c)
    slow = _tril_mask(c, strict=True)
    eye = (lax.broadcasted_iota(jnp.int32, (c, c), 0)
           == lax.broadcasted_iota(jnp.int32, (c, c), 1)).astype(F32)

    for h in range(heads):
        sl = slice(h * n, (h + 1) * n)
        kk_h = kk[:, sl]
        nrm = jnp.sqrt(jnp.sum(kk_h * kk_h, axis=-1, keepdims=True))
        kk_h = kk_h / jnp.maximum(nrm, 1e-12)
        b_h = kk_h * a[:, sl]
        k_h = k2[:, sl]
        v_h = v[:, sl]
        r_h = r[:, sl]
        at = -kk_h * e_prev[:, sl]
        rt = r_h * e_pos[:, sl]
        bt = b_h * e_neg[:, sl]
        kt = k_h * e_neg[:, sl]
        lhs = jnp.concatenate([at, rt], axis=0)
        rhs = jnp.concatenate([bt, kt], axis=0)
        amat = _dot_nt(lhs, rhs)
        a_ab = jnp.where(slow, amat[0:c, 0:c], 0.0)
        a_ak = jnp.where(slow, amat[0:c, c:2 * c], 0.0)
        a_rb = jnp.where(low, amat[c:2 * c, 0:c], 0.0)
        a_rk = jnp.where(low, amat[c:2 * c, c:2 * c], 0.0)

        s0 = s_ref[h]
        ls = _dot_nt(lhs, s0)

        pw = a_ab
        tinv = eye + a_ab
        steps = max(1, (c - 1).bit_length()) - 1
        for _ in range(steps):
            pw = _dot(pw, pw)
            tinv = tinv + _dot(tinv, pw)

        u = _dot(tinv, ls[0:c] + _dot(a_ak, v_h))
        uv = jnp.concatenate([u, v_h], axis=0)
        y = ls[c:2 * c] + _dot(jnp.concatenate([a_rb, a_rk], axis=1), uv)

        end_h = e_end[:, sl]
        bk_end = jnp.concatenate([b_h * end_h, k_h * end_h], axis=0)
        s_ref[h] = s0 * e_pos[c - 1:c, sl] + _dot_tn(uv, bk_end)

        mean = jnp.mean(y, axis=-1, keepdims=True)
        var = jnp.mean(jnp.square(y - mean), axis=-1, keepdims=True)
        yn = (y - mean) * lax.rsqrt(var + A_GN_EPS)
        yn = yn * lnw_ref[:, sl] + lnb_ref[:, sl]
        bonus = jnp.sum(r_h * k_h * rk_ref[:, sl], axis=-1, keepdims=True) * v_h
        o_ref[:, sl] = ((yn + bonus) * g[:, sl]).astype(o_ref.dtype)


def _rwkv_mixer(p, mu, w0, w2, a0, a2, g2, k_k, k_a, r_k, ln_w, ln_b):
    bsz, lp, cols = p.shape
    width = w0.shape[-1]
    heads = width // A_HEAD_DIM
    kern = functools.partial(_rwkv_kernel, width=width, d_rank=w2.shape[0],
                             a_rank=a2.shape[0], g_rank=g2.shape[0])
    full = lambda arr: pl.BlockSpec(arr.shape, lambda b, c: (0,) * arr.ndim)
    small = (mu, w0, w2, a0, a2, g2, k_k, k_a, r_k, ln_w, ln_b)
    return pl.pallas_call(
        kern,
        grid=(bsz, lp // CHUNK),
        in_specs=[pl.BlockSpec((None, CHUNK, cols), lambda b, c: (b, c, 0))]
        + [full(t) for t in small],
        out_specs=pl.BlockSpec((None, CHUNK, width), lambda b, c: (b, c, 0)),
        out_shape=jax.ShapeDtypeStruct((bsz, lp, width), BF16),
        scratch_shapes=[pltpu.VMEM((heads, A_HEAD_DIM, A_HEAD_DIM), F32),
                        pltpu.VMEM((1, cols), F32)],
        compiler_params=_params(2),
    )(p, *small)


def _gla_kernel(p_ref, a2_ref, ab_ref, nw_ref, o_ref, s_ref, *, qk_width, v_width, rank):
    c = CHUNK
    dv = B_DV
    heads = v_width // dv
    dk = qk_width // heads

    @pl.when(pl.program_id(1) == 0)
    def _():
        s_ref[...] = jnp.zeros_like(s_ref)

    off_v = 2 * qk_width
    off_a = off_v + v_width
    off_g = off_a + _rup(rank, LANE)
    p = p_ref[...]
    q = p[:, 0:qk_width] * (dk ** -0.5)
    k = p[:, qk_width:off_v]
    v = p[:, off_v:off_a]
    xa = p[:, off_a:off_a + rank]
    gate = p[:, off_g:off_g + v_width]

    lg = _log_sigmoid(_dot(xa, a2_ref[...]) + ab_ref[...]) / B_GATE_NORMALIZER
    tri = _tril_mask(c).astype(F32)
    bcum = _dot(tri, lg)
    low = _tril_mask(c)

    for h in range(heads):
        sl = slice(h * dk, (h + 1) * dk)
        vs = slice(h * dv, (h + 1) * dv)
        b_h = bcum[:, sl]
        q_h = q[:, sl]
        k_h = k[:, sl]
        v_h = v[:, vs]
        st = s_ref[h]
        o_inter = _dot_nt(q_h * jnp.exp(b_h), st)

        blocks = []
        for i0 in range(0, c, B_SUB):
            i1 = i0 + B_SUB
            ref_pt = b_h[i1 - 1:i1, :]
            qf = q_h[i0:i1] * jnp.exp(b_h[i0:i1] - ref_pt)
            kf = k_h[0:i1] * jnp.exp(ref_pt - b_h[0:i1])
            sc = jnp.where(low[i0:i1, 0:i1], _dot_nt(qf, kf), 0.0)
            blocks.append(_dot(sc, v_h[0:i1]))
        o = o_inter + jnp.concatenate(blocks, axis=0)

        b_last = b_h[c - 1:c, :]
        s_ref[h] = st * jnp.exp(b_last) + _dot_tn(v_h, k_h * jnp.exp(b_last - b_h))

        o = o * lax.rsqrt(jnp.mean(o * o, axis=-1, keepdims=True) + EPS) * nw_ref[...]
        g_h = gate[:, vs]
        o_ref[:, vs] = (o * (g_h * jax.nn.sigmoid(g_h))).astype(o_ref.dtype)


def _gla_mixer(p, a2, ab, norm_w):
    bsz, lp, cols = p.shape
    rank, qk_width = a2.shape
    dv = norm_w.shape[-1]
    heads = qk_width // (dv // 2)
    v_width = heads * dv
    kern = functools.partial(_gla_kernel, qk_width=qk_width, v_width=v_width, rank=rank)
    full = lambda arr: pl.BlockSpec(arr.shape, lambda b, c: (0,) * arr.ndim)
    small = (a2, ab, norm_w)
    return pl.pallas_call(
        kern,
        grid=(bsz, lp // CHUNK),
        in_specs=[pl.BlockSpec((None, CHUNK, cols), lambda b, c: (b, c, 0))]
        + [full(t) for t in small],
        out_specs=pl.BlockSpec((None, CHUNK, v_width), lambda b, c: (b, c, 0)),
        out_shape=jax.ShapeDtypeStruct((bsz, lp, v_width), BF16),
        scratch_shapes=[pltpu.VMEM((heads, dv, qk_width // heads), F32)],
        compiler_params=_params(2),
    )(p, *small)


def _mlstm_kernel(p_ref, cw_ref, cb_ref, gb_ref, nw_ref, o_ref, cm_ref, n_ref, m_ref,
                  hist_ref, *, qk_width, v_width, heads):
    c = CHUNK
    dqk = qk_width // heads
    dv = v_width // heads

    @pl.when(pl.program_id(1) == 0)
    def _():
        cm_ref[...] = jnp.zeros_like(cm_ref)
        n_ref[...] = jnp.zeros_like(n_ref)
        m_ref[...] = jnp.zeros_like(m_ref)
        hist_ref[...] = jnp.zeros_like(hist_ref)

    off_v = 2 * qk_width
    off_g = off_v + v_width
    off_o = off_g + LANE
    p = p_ref[...]
    qk_raw = p[:, 0:off_v]
    v = p[:, off_v:off_g]
    gates = p[:, off_g:off_o]
    og = p[:, off_o:off_o + v_width]

    ext = jnp.concatenate([hist_ref[...], qk_raw], axis=0)
    hist_ref[...] = qk_raw[c - 8:c, :]
    conv = cb_ref[...] + ext * cw_ref[C_CONV - 1:C_CONV, :]
    for d in range(1, C_CONV):
        conv = conv + pltpu.roll(ext, d, axis=0) * cw_ref[C_CONV - 1 - d:C_CONV - d, :]
    conv = conv[8:, :]
    qk = conv * jax.nn.sigmoid(conv)
    q = qk[:, 0:qk_width]
    k = qk[:, qk_width:off_v] * (dqk ** -0.5)

    capped = C_GATE_CAP * jnp.tanh((gates + gb_ref[...]) / C_GATE_CAP)
    lf = _log_sigmoid(capped)
    tri = _tril_mask(c).astype(F32)
    bcol = _dot(tri, lf)
    brow = bcol.T
    irow = capped.T
    low = _tril_mask(c)

    for h in range(heads):
        qs = slice(h * dqk, (h + 1) * dqk)
        vs = slice(h * dv, (h + 1) * dv)
        q_h = q[:, qs]
        k_h = k[:, qs]
        v_h = v[:, vs]
        bc = bcol[:, heads + h:heads + h + 1]
        br = brow[heads + h:heads + h + 1, :]
        ic = capped[:, h:h + 1]
        ir = irow[h:h + 1, :]
        m_prev = m_ref[h:h + 1, 0:1]
        cm = cm_ref[h]
        n_row = n_ref[h:h + 1, :]

        log_w = jnp.where(low, bc - br + ir, -jnp.inf)
        log_prev = bc + m_prev
        m_t = jnp.maximum(log_prev, jnp.max(log_w, axis=-1, keepdims=True))
        w = jnp.exp(log_w - m_t)
        w_prev = jnp.exp(log_prev - m_t)
        s = _dot_nt(q_h, k_h) * w
        num = w_prev * _dot(q_h, cm) + _dot(s, v_h)
        den = (w_prev * jnp.sum(q_h * n_row, axis=-1, keepdims=True)
               + jnp.sum(s, axis=-1, keepdims=True))
        hh = num / jnp.maximum(jnp.abs(den), jnp.exp(-m_t))

        m_new = m_t[c - 1:c, :]
        b_last = bc[c - 1:c, :]
        w_end = jnp.exp(b_last - bc + ic - m_new)
        f_end = jnp.exp(b_last + m_prev - m_new)
        kw = k_h * w_end
        cm_ref[h] = f_end * cm + _dot_tn(kw, v_h)
        n_ref[h:h + 1, :] = f_end * n_row + jnp.sum(kw, axis=0, keepdims=True)
        m_ref[h:h + 1, :] = jnp.broadcast_to(m_new, (1, m_ref.shape[1]))

        hh = hh * lax.rsqrt(jnp.mean(hh * hh, axis=-1, keepdims=True) + EPS)
        o_ref[:, vs] = (hh * nw_ref[:, vs] * jax.nn.sigmoid(og[:, vs])).astype(o_ref.dtype)


def _mlstm_mixer(p, conv_w, conv_b, gate_b, norm_w, heads):
    bsz, lp, cols = p.shape
    qk2 = conv_w.shape[-1]
    qk_width = qk2 // 2
    v_width = norm_w.shape[-1]
    kern = functools.partial(_mlstm_kernel, qk_width=qk_width, v_width=v_width, heads=heads)
    full = lambda arr: pl.BlockSpec(arr.shape, lambda b, c: (0,) * arr.ndim)
    small = (conv_w, conv_b, gate_b, norm_w)
    return pl.pallas_call(
        kern,
        grid=(bsz, lp // CHUNK),
        in_specs=[pl.BlockSpec((None, CHUNK, cols), lambda b, c: (b, c, 0))]
        + [full(t) for t in small],
        out_specs=pl.BlockSpec((None, CHUNK, v_width), lambda b, c: (b, c, 0)),
        out_shape=jax.ShapeDtypeStruct((bsz, lp, v_width), BF16),
        scratch_shapes=[pltpu.VMEM((heads, qk_width // heads, v_width // heads), F32),
                        pltpu.VMEM((8, qk_width // heads), F32),
                        pltpu.VMEM((8, LANE), F32),
                        pltpu.VMEM((8, qk2), F32)],
        compiler_params=_params(2),
    )(p, *small)


def _out_proj_kernel(h_ref, ya_ref, yb_ref, yc_ref, wa_ref, wb_ref, wc_ref, o_ref):
    acc = jnp.dot(ya_ref[...], wa_ref[...], preferred_element_type=F32)
    acc += jnp.dot(yb_ref[...], wb_ref[...], preferred_element_type=F32)
    acc += jnp.dot(yc_ref[...], wc_ref[...], preferred_element_type=F32)
    o_ref[...] = h_ref[...] + acc


def _out_proj(h, ya, yb, yc, wa, wb, wc, tm):
    m, d = h.shape
    row = lambda arr: pl.BlockSpec((tm, arr.shape[1]), lambda i: (i, 0))
    full = lambda arr: pl.BlockSpec(arr.shape, lambda i: (0, 0))
    return pl.pallas_call(
        _out_proj_kernel,
        grid=(m // tm,),
        in_specs=[row(h), row(ya), row(yb), row(yc), full(wa), full(wb), full(wc)],
        out_specs=pl.BlockSpec((tm, d), lambda i: (i, 0)),
        out_shape=jax.ShapeDtypeStruct((m, d), F32),
        compiler_params=_params(1),
    )(h, ya, yb, yc, wa, wb, wc)


def _ffn_kernel(h_ref, g_ref, w1_ref, w3_ref, w2_ref, gf_ref, o_ref, u_ref, *, final_norm):
    f = pl.program_id(1)

    @pl.when(f == 0)
    def _():
        x = h_ref[...]
        ms = jnp.mean(x * x, axis=-1, keepdims=True)
        u_ref[...] = (x * lax.rsqrt(ms + EPS) * g_ref[...]).astype(BF16)
        o_ref[...] = x

    u = u_ref[...]
    a = jnp.dot(u, w1_ref[...], preferred_element_type=F32)
    b = jnp.dot(u, w3_ref[...], preferred_element_type=F32)
    act = (a * jax.nn.sigmoid(a) * b).astype(BF16)
    o_ref[...] += jnp.dot(act, w2_ref[...], preferred_element_type=F32)

    if final_norm:
        @pl.when(f == pl.num_programs(1) - 1)
        def _():
            y = o_ref[...]
            ms = jnp.mean(y * y, axis=-1, keepdims=True)
            o_ref[...] = y * lax.rsqrt(ms + EPS) * gf_ref[...]


def _ffn(h, g, w1, w3, w2, g_final, tm, tf, final_norm):
    m, d = h.shape
    dff = w1.shape[1]
    return pl.pallas_call(
        functools.partial(_ffn_kernel, final_norm=final_norm),
        grid=(m // tm, dff // tf),
        in_specs=[
            pl.BlockSpec((tm, d), lambda i, f: (i, 0)),
            pl.BlockSpec((1, d), lambda i, f: (0, 0)),
            pl.BlockSpec((d, tf), lambda i, f: (0, f)),
            pl.BlockSpec((d, tf), lambda i, f: (0, f)),
            pl.BlockSpec((tf, d), lambda i, f: (f, 0)),
            pl.BlockSpec((1, d), lambda i, f: (0, 0)),
        ],
        out_specs=pl.BlockSpec((tm, d), lambda i, f: (i, 0)),
        out_shape=jax.ShapeDtypeStruct((m, d), F32),
        scratch_shapes=[pltpu.VMEM((tm, d), BF16)],
        compiler_params=_params(2),
    )(h, g, w1, w3, w2, g_final)


def _pad_groups(w, sizes, axis=-1):
    pieces = []
    off = 0
    for s in sizes:
        piece = lax.slice_in_dim(w, off, off + s, axis=axis)
        pad = _rup(s, LANE) - s
        if pad:
            cfg = [(0, 0)] * w.ndim
            cfg[axis] = (0, pad)
            piece = jnp.pad(piece, cfg)
        pieces.append(piece)
        off += s
    return jnp.concatenate(pieces, axis=axis)


def _tile_rows(m):
    for t in (640, 512, 320, 256, 128, 64, 32, 16, 8):
        if m % t == 0:
            return t
    return m


def _tile_cols(n, cap):
    best = LANE
    for t in range(LANE, cap + 1, LANE):
        if n % t == 0:
            best = t
    return best


def kernel(x, meta_tokens, norm_mix, w_in, rw_mu, rw_w0, rw_w2, rw_a0, rw_a2, rw_g2, rw_kk, rw_ka, rw_rk, rw_ln_w, rw_ln_b, gla_a2, gla_ab, gla_norm, ml_conv_w, ml_conv_b, ml_ib, ml_fb, ml_norm, w_out, norm_ffn, ffn_w1, ffn_w3, ffn_w2, norm_final):
    bsz, seq, d = x.shape
    depth = w_in.shape[0]
    n_meta = meta_tokens.shape[0]
    ltot = n_meta + seq
    lp = _rup(ltot, CHUNK)

    a_width = rw_w0.shape[-1]
    d_rank, a_rank, g_rank = rw_w2.shape[1], rw_a2.shape[1], rw_g2.shape[1]
    a_sizes = [a_width, a_width, a_width, d_rank, a_rank, g_rank]
    b_rank, b_qk = gla_a2.shape[1], gla_a2.shape[2]
    b_heads = b_qk // (gla_norm.shape[-1] // 2)
    b_width = b_heads * gla_norm.shape[-1]
    b_sizes = [b_qk, b_qk, b_width, b_rank, b_width]
    c_heads = ml_ib.shape[-1]
    c_qk = ml_conv_w.shape[-1] // 2
    c_width = ml_norm.shape[-1]
    c_sizes = [c_qk, c_qk, c_width, 2 * c_heads, c_width]
    a_cols, b_cols = sum(a_sizes), sum(b_sizes)

    meta = jnp.broadcast_to(meta_tokens[None].astype(x.dtype), (bsz, n_meta, d))
    h = jnp.concatenate([meta, x, jnp.zeros((bsz, lp - ltot, d), x.dtype)], axis=1)
    h = h.reshape(bsz * lp, d)
    m = bsz * lp
    tm = _tile_rows(m)
    row2 = lambda t: t.reshape(1, -1)

    for l in range(depth):
        w_l = w_in[l]
        wa = _pad_groups(w_l[:, :a_cols], a_sizes).astype(BF16)
        wb = _pad_groups(w_l[:, a_cols:a_cols + b_cols], b_sizes).astype(BF16)
        wc = _pad_groups(w_l[:, a_cols + b_cols:], c_sizes).astype(BF16)
        g_mix = row2(norm_mix[l])
        p_a = _norm_matmul(h, g_mix, wa, tm, _tile_cols(wa.shape[1], 2048))
        p_b = _norm_matmul(h, g_mix, wb, tm, _tile_cols(wb.shape[1], 2048))
        p_c = _norm_matmul(h, g_mix, wc, tm, _tile_cols(wc.shape[1], 2048))

        y_a = _rwkv_mixer(
            p_a.reshape(bsz, lp, -1), _pad_groups(row2(rw_mu[l]), a_sizes),
            row2(rw_w0[l]), rw_w2[l], row2(rw_a0[l]), rw_a2[l], rw_g2[l], row2(rw_kk[l]),
            row2(rw_ka[l]), row2(rw_rk[l]), row2(rw_ln_w[l]), row2(rw_ln_b[l]))
        y_b = _gla_mixer(p_b.reshape(bsz, lp, -1), gla_a2[l], row2(gla_ab[l]),
                         row2(gla_norm[l]))
        gate_b = jnp.pad(jnp.concatenate([ml_ib[l], ml_fb[l]]), (0, LANE - 2 * c_heads))
        y_c = _mlstm_mixer(p_c.reshape(bsz, lp, -1), ml_conv_w[l], row2(ml_conv_b[l]),
                           row2(gate_b), row2(ml_norm[l]), c_heads)

        wo = w_out[l].astype(BF16)
        h = _out_proj(h, y_a.reshape(m, -1), y_b.reshape(m, -1), y_c.reshape(m, -1),
                      wo[:a_width], wo[a_width:a_width + b_width], wo[a_width + b_width:], tm)
        last = l == depth - 1
        h = _ffn(h, row2(norm_ffn[l]), ffn_w1[l].astype(BF16), ffn_w3[l].astype(BF16),
                 ffn_w2[l].astype(BF16), row2(norm_final), tm,
                 _tile_cols(ffn_w1.shape[-1], 512), last)

    return h.reshape(bsz, lp, d)[:, n_meta:ltot]
```

```python
import functools

import jax
import jax.numpy as jnp
from jax import lax
from jax.experimental import pallas as pl
from jax.experimental.pallas import tpu as pltpu

F32 = jnp.float32
BF16 = jnp.bfloat16

EPS = 1e-6
N_META = 16
CHUNK = 64

A_HEAD_DIM = 64
A_GN_EPS = 64e-5
B_DV = 128
B_GATE_NORMALIZER = 16.0
B_SUB = 16
C_DV = 256
C_CONV = 4
C_GATE_CAP = 15.0

LANE = 128
VMEM_LIMIT = 56 * 1024 * 1024

NN = (((1,), (0,)), ((), ()))
NT = (((1,), (1,)), ((), ()))
TN = (((0,), (0,)), ((), ()))

GATE_PASSES = 3
A_PASSES = 1
A_INV_PASSES = 1
B_PASSES = 1
C_PASSES = 1


def _rup(n, m):
    return -(-n // m) * m


def _split(x, parts):
    out = []
    rem = x
    for i in range(parts):
        hi = rem.astype(BF16)
        out.append(hi)
        if i + 1 < parts:
            rem = rem - hi.astype(F32)
    return out


def _mm_parts(a_parts, b_parts, dims):
    dg = lambda a, b: lax.dot_general(a, b, dims, preferred_element_type=F32)
    acc = dg(a_parts[0], b_parts[0])
    if len(a_parts) > 1:
        acc = acc + dg(a_parts[0], b_parts[1]) + dg(a_parts[1], b_parts[0])
    return acc


def _mm(a, b, dims=NN, passes=1):
    n = 2 if passes == 3 else 1
    return _mm_parts(_split(a, n), _split(b, n), dims)


def _cumsum_rows(x):
    tri = _tril_mask(x.shape[0]).astype(BF16)
    hi, mid, lo = (jnp.dot(tri, part, preferred_element_type=F32) for part in _split(x, 3))
    return hi + (mid + lo)


def _softplus(z):
    return jnp.maximum(z, 0.0) + jnp.log1p(jnp.exp(-jnp.abs(z)))


def _log_sigmoid(z):
    return -_softplus(-z)


def _tril_mask(n, strict=False):
    r = lax.broadcasted_iota(jnp.int32, (n, n), 0)
    c = lax.broadcasted_iota(jnp.int32, (n, n), 1)
    return (r > c) if strict else (r >= c)


def _params(n_axes):
    return pltpu.CompilerParams(
        dimension_semantics=("arbitrary",) * n_axes, vmem_limit_bytes=VMEM_LIMIT)


def _norm_matmul_kernel(h_ref, g_ref, w_ref, o_ref, u_ref):
    @pl.when(pl.program_id(1) == 0)
    def _():
        x = h_ref[...]
        ms = jnp.mean(x * x, axis=-1, keepdims=True)
        u_ref[...] = (x * lax.rsqrt(ms + EPS) * g_ref[...]).astype(BF16)

    o_ref[...] = jnp.dot(u_ref[...], w_ref[...], preferred_element_type=F32)


def _norm_matmul(h, g, w, tm, tn):
    m, d = h.shape
    n = w.shape[1]
    return pl.pallas_call(
        _norm_matmul_kernel,
        grid=(m // tm, n // tn),
        in_specs=[
            pl.BlockSpec((tm, d), lambda i, j: (i, 0)),
            pl.BlockSpec((1, d), lambda i, j: (0, 0)),
            pl.BlockSpec((d, tn), lambda i, j: (0, j)),
        ],
        out_specs=pl.BlockSpec((tm, tn), lambda i, j: (i, j)),
        out_shape=jax.ShapeDtypeStruct((m, n), F32),
        scratch_shapes=[pltpu.VMEM((tm, d), BF16)],
        compiler_params=_params(2),
    )(h, g, w)


def _rwkv_kernel(p_ref, mu_ref, w0_ref, w2_ref, a0_ref, a2_ref, g2_ref, kk_ref, ka_ref,
                 rk_ref, lnw_ref, lnb_ref, o_ref, s_ref, prev_ref, *, width, d_rank,
                 a_rank, g_rank):
    c = CHUNK
    n = A_HEAD_DIM
    heads = width // n

    @pl.when(pl.program_id(1) == 0)
    def _():
        s_ref[...] = jnp.zeros_like(s_ref)
        prev_ref[...] = jnp.zeros_like(prev_ref)

    p = p_ref[...]
    row = lax.broadcasted_iota(jnp.int32, p.shape, 0)
    shifted = jnp.where(row == 0, prev_ref[...], pltpu.roll(p, 1, axis=0))
    prev_ref[...] = p[c - 1:c, :]
    x = p + mu_ref[...] * (shifted - p)

    off_w = 3 * width
    off_a = off_w + _rup(d_rank, LANE)
    off_g = off_a + _rup(a_rank, LANE)
    r = x[:, 0:width]
    k = x[:, width:2 * width]
    v = x[:, 2 * width:3 * width]
    xw = x[:, off_w:off_w + d_rank]
    xa = x[:, off_a:off_a + a_rank]
    xg = x[:, off_g:off_g + g_rank]

    w_log = -_softplus(-(w0_ref[...] + _mm(jnp.tanh(xw), w2_ref[...], NN, GATE_PASSES))) - 0.5
    ld = -jnp.exp(w_log)
    a = jax.nn.sigmoid(a0_ref[...] + _mm(xa, a2_ref[...], NN, GATE_PASSES))
    g = _mm(jax.nn.sigmoid(xg), g2_ref[...], NN, GATE_PASSES)
    kk = k * kk_ref[...]
    k2 = k * (1.0 + (a - 1.0) * ka_ref[...])

    cum = _cumsum_rows(ld)
    e_pos = jnp.exp(cum)
    e_neg = jnp.exp(-cum)
    e_prev = jnp.exp(cum - ld)
    e_end = jnp.exp(cum[c - 1:c, :] - cum)

    low = _tril_mask(c)
    slow = _tril_mask(c, strict=True)
    eye = (lax.broadcasted_iota(jnp.int32, (c, c), 0)
           == lax.broadcasted_iota(jnp.int32, (c, c), 1)).astype(F32)
    hs = range(heads)
    sls = [slice(h * n, (h + 1) * n) for h in hs]
    np_ = 2 if A_PASSES == 3 else 1
    ni_ = 2 if A_INV_PASSES == 3 else 1

    kk_n, b_h, lhs, rhs = [], [], [], []
    for sl in sls:
        kk_h = kk[:, sl]
        nrm = jnp.sqrt(jnp.sum(kk_h * kk_h, axis=-1, keepdims=True))
        kk_h = kk_h / jnp.maximum(nrm, 1e-12)
        kk_n.append(kk_h)
        b_h.append(kk_h * a[:, sl])
        lhs.append(_split(jnp.concatenate(
            [-kk_h * e_prev[:, sl], r[:, sl] * e_pos[:, sl]], axis=0), np_))
        rhs.append(_split(jnp.concatenate(
            [b_h[-1] * e_neg[:, sl], k2[:, sl] * e_neg[:, sl]], axis=0), np_))

    amat = [_mm_parts(lhs[h], rhs[h], NT) for h in hs]
    s0 = [s_ref[h] for h in hs]
    ls = [_mm_parts(lhs[h], _split(s0[h], np_), NT) for h in hs]

    a_ab = [jnp.where(slow, amat[h][0:c, 0:c], 0.0) for h in hs]
    rhs_u = [ls[h][0:c] + _mm(jnp.where(slow, amat[h][0:c, c:2 * c], 0.0), v[:, sls[h]],
                              NN, A_PASSES) for h in hs]

    pw = [_split(a_ab[h], ni_) for h in hs]
    tinv = [eye + a_ab[h] for h in hs]
    for _ in range(max(1, (c - 1).bit_length()) - 1):
        pw_f = [_mm_parts(pw[h], pw[h], NN) for h in hs]
        pw = [_split(pw_f[h], ni_) for h in hs]
        tinv = [tinv[h] + _mm_parts(_split(tinv[h], ni_), pw[h], NN) for h in hs]

    u = [_mm(tinv[h], rhs_u[h], NN, A_INV_PASSES) for h in hs]
    uv = [_split(jnp.concatenate([u[h], v[:, sls[h]]], axis=0), np_) for h in hs]
    y = []
    for h in hs:
        a_r = jnp.concatenate([jnp.where(low, amat[h][c:2 * c, 0:c], 0.0),
                               jnp.where(low, amat[h][c:2 * c, c:2 * c], 0.0)], axis=1)
        y.append(ls[h][c:2 * c] + _mm_parts(_split(a_r, np_), uv[h], NN))

    for h in hs:
        sl = sls[h]
        end_h = e_end[:, sl]
        bk_end = jnp.concatenate([b_h[h] * end_h, k2[:, sl] * end_h], axis=0)
        s_ref[h] = s0[h] * e_pos[c - 1:c, sl] + _mm_parts(uv[h], _split(bk_end, np_), TN)

    for h in hs:
        sl = sls[h]
        mean = jnp.mean(y[h], axis=-1, keepdims=True)
        var = jnp.mean(jnp.square(y[h] - mean), axis=-1, keepdims=True)
        yn = (y[h] - mean) * lax.rsqrt(var + A_GN_EPS)
        yn = yn * lnw_ref[:, sl] + lnb_ref[:, sl]
        bonus = jnp.sum(r[:, sl] * k2[:, sl] * rk_ref[:, sl], axis=-1, keepdims=True) * v[:, sl]
        o_ref[:, sl] = ((yn + bonus) * g[:, sl]).astype(o_ref.dtype)


def _rwkv_mixer(p, mu, w0, w2, a0, a2, g2, k_k, k_a, r_k, ln_w, ln_b, out_dtype=BF16):
    bsz, lp, cols = p.shape
    width = w0.shape[-1]
    heads = width // A_HEAD_DIM
    kern = functools.partial(_rwkv_kernel, width=width, d_rank=w2.shape[0],
                             a_rank=a2.shape[0], g_rank=g2.shape[0])
    full = lambda arr: pl.BlockSpec(arr.shape, lambda b, c: (0,) * arr.ndim)
    small = (mu, w0, w2, a0, a2, g2, k_k, k_a, r_k, ln_w, ln_b)
    return pl.pallas_call(
        kern,
        grid=(bsz, lp // CHUNK),
        in_specs=[pl.BlockSpec((None, CHUNK, cols), lambda b, c: (b, c, 0))]
        + [full(t) for t in small],
        out_specs=pl.BlockSpec((None, CHUNK, width), lambda b, c: (b, c, 0)),
        out_shape=jax.ShapeDtypeStruct((bsz, lp, width), out_dtype),
        scratch_shapes=[pltpu.VMEM((heads, A_HEAD_DIM, A_HEAD_DIM), F32),
                        pltpu.VMEM((1, cols), F32)],
        compiler_params=_params(2),
    )(p, *small)


def _gla_kernel(p_ref, a2_ref, ab_ref, nw_ref, o_ref, s_ref, *, qk_width, v_width, rank):
    c = CHUNK
    dv = B_DV
    heads = v_width // dv
    dk = qk_width // heads

    @pl.when(pl.program_id(1) == 0)
    def _():
        s_ref[...] = jnp.zeros_like(s_ref)

    off_v = 2 * qk_width
    off_a = off_v + v_width
    off_g = off_a + _rup(rank, LANE)
    p = p_ref[...]
    q = p[:, 0:qk_width] * (dk ** -0.5)
    k = p[:, qk_width:off_v]
    v = p[:, off_v:off_a]
    xa = p[:, off_a:off_a + rank]
    gate = p[:, off_g:off_g + v_width]

    lg = _log_sigmoid(_mm(xa, a2_ref[...], NN, GATE_PASSES) + ab_ref[...]) / B_GATE_NORMALIZER
    bcum = _cumsum_rows(lg)
    low = _tril_mask(c)
    b_last = bcum[c - 1:c, :]
    q_in = q * jnp.exp(bcum)
    k_end = k * jnp.exp(b_last - bcum)
    e_last = jnp.exp(b_last)

    qf, kf = [], []
    for i0 in range(0, c, B_SUB):
        i1 = i0 + B_SUB
        ref_pt = bcum[i1 - 1:i1, :]
        qf.append(q[i0:i1] * jnp.exp(bcum[i0:i1] - ref_pt))
        kf.append(k[0:i1] * jnp.exp(ref_pt - bcum[0:i1]))

    hs = range(heads)
    sls = [slice(h * dk, (h + 1) * dk) for h in hs]
    vss = [slice(h * dv, (h + 1) * dv) for h in hs]
    st = [s_ref[h] for h in hs]
    o = [_mm(q_in[:, sls[h]], st[h], NT, B_PASSES) for h in hs]
    for h in hs:
        blocks = []
        for bi, i0 in enumerate(range(0, c, B_SUB)):
            i1 = i0 + B_SUB
            sc = _mm(qf[bi][:, sls[h]], kf[bi][:, sls[h]], NT, B_PASSES)
            sc = jnp.where(low[i0:i1, 0:i1], sc, 0.0)
            blocks.append(_mm(sc, v[0:i1, vss[h]], NN, B_PASSES))
        o[h] = o[h] + jnp.concatenate(blocks, axis=0)
    for h in hs:
        s_ref[h] = st[h] * e_last[:, sls[h]] + _mm(v[:, vss[h]], k_end[:, sls[h]], TN, B_PASSES)
    for h in hs:
        vs = vss[h]
        o_h = o[h] * lax.rsqrt(jnp.mean(o[h] * o[h], axis=-1, keepdims=True) + EPS) * nw_ref[...]
        g_h = gate[:, vs]
        o_ref[:, vs] = (o_h * (g_h * jax.nn.sigmoid(g_h))).astype(o_ref.dtype)


def _gla_mixer(p, a2, ab, norm_w, out_dtype=BF16):
    bsz, lp, cols = p.shape
    rank, qk_width = a2.shape
    dv = norm_w.shape[-1]
    heads = qk_width // (dv // 2)
    v_width = heads * dv
    kern = functools.partial(_gla_kernel, qk_width=qk_width, v_width=v_width, rank=rank)
    full = lambda arr: pl.BlockSpec(arr.shape, lambda b, c: (0,) * arr.ndim)
    small = (a2, ab, norm_w)
    return pl.pallas_call(
        kern,
        grid=(bsz, lp // CHUNK),
        in_specs=[pl.BlockSpec((None, CHUNK, cols), lambda b, c: (b, c, 0))]
        + [full(t) for t in small],
        out_specs=pl.BlockSpec((None, CHUNK, v_width), lambda b, c: (b, c, 0)),
        out_shape=jax.ShapeDtypeStruct((bsz, lp, v_width), out_dtype),
        scratch_shapes=[pltpu.VMEM((heads, dv, qk_width // heads), F32)],
        compiler_params=_params(2),
    )(p, *small)


def _mlstm_kernel(p_ref, cw_ref, cb_ref, gb_ref, nw_ref, o_ref, cm_ref, n_ref, m_ref,
                  hist_ref, *, qk_width, v_width, heads):
    c = CHUNK
    dqk = qk_width // heads
    dv = v_width // heads

    @pl.when(pl.program_id(1) == 0)
    def _():
        cm_ref[...] = jnp.zeros_like(cm_ref)
        n_ref[...] = jnp.zeros_like(n_ref)
        m_ref[...] = jnp.zeros_like(m_ref)
        hist_ref[...] = jnp.zeros_like(hist_ref)

    off_v = 2 * qk_width
    off_g = off_v + v_width
    off_o = off_g + LANE
    p = p_ref[...]
    qk_raw = p[:, 0:off_v]
    v = p[:, off_v:off_g]
    gates = p[:, off_g:off_o]
    og = p[:, off_o:off_o + v_width]

    ext = jnp.concatenate([hist_ref[...], qk_raw], axis=0)
    hist_ref[...] = qk_raw[c - 8:c, :]
    conv = cb_ref[...] + ext * cw_ref[C_CONV - 1:C_CONV, :]
    for d in range(1, C_CONV):
        conv = conv + pltpu.roll(ext, d, axis=0) * cw_ref[C_CONV - 1 - d:C_CONV - d, :]
    conv = conv[8:, :]
    qk = conv * jax.nn.sigmoid(conv)
    q = qk[:, 0:qk_width]
    k = qk[:, qk_width:off_v] * (dqk ** -0.5)

    capped = C_GATE_CAP * jnp.tanh((gates + gb_ref[...]) / C_GATE_CAP)
    lf = _log_sigmoid(capped)
    bcol = _cumsum_rows(lf)
    brow = bcol.T
    irow = capped.T
    low = _tril_mask(c)
    hs = range(heads)
    qss = [slice(h * dqk, (h + 1) * dqk) for h in hs]
    vss = [slice(h * dv, (h + 1) * dv) for h in hs]
    np_ = 2 if C_PASSES == 3 else 1

    q_p = [_split(q[:, qss[h]], np_) for h in hs]
    k_p = [_split(k[:, qss[h]], np_) for h in hs]
    cm = [cm_ref[h] for h in hs]
    qk_s = [_mm_parts(q_p[h], k_p[h], NT) for h in hs]
    q_cm = [_mm_parts(q_p[h], _split(cm[h], np_), NN) for h in hs]

    hh = []
    for h in hs:
        bc = bcol[:, heads + h:heads + h + 1]
        br = brow[heads + h:heads + h + 1, :]
        ic = capped[:, h:h + 1]
        ir = irow[h:h + 1, :]
        m_prev = m_ref[h:h + 1, 0:1]
        n_row = n_ref[h:h + 1, :]

        log_w = jnp.where(low, bc - br + ir, -jnp.inf)
        log_prev = bc + m_prev
        m_t = jnp.maximum(log_prev, jnp.max(log_w, axis=-1, keepdims=True))
        w = jnp.exp(log_w - m_t)
        w_prev = jnp.exp(log_prev - m_t)
        s = qk_s[h] * w
        num = w_prev * q_cm[h] + _mm(s, v[:, vss[h]], NN, C_PASSES)
        den = (w_prev * jnp.sum(q[:, qss[h]] * n_row, axis=-1, keepdims=True)
               + jnp.sum(s, axis=-1, keepdims=True))
        hh.append(num / jnp.maximum(jnp.abs(den), jnp.exp(-m_t)))

        m_new = m_t[c - 1:c, :]
        b_last = bc[c - 1:c, :]
        w_end = jnp.exp(b_last - bc + ic - m_new)
        f_end = jnp.exp(b_last + m_prev - m_new)
        kw = k[:, qss[h]] * w_end
        cm_ref[h] = f_end * cm[h] + _mm(kw, v[:, vss[h]], TN, C_PASSES)
        n_ref[h:h + 1, :] = f_end * n_row + jnp.sum(kw, axis=0, keepdims=True)
        m_ref[h:h + 1, :] = jnp.broadcast_to(m_new, (1, m_ref.shape[1]))

    for h in hs:
        vs = vss[h]
        hn = hh[h] * lax.rsqrt(jnp.mean(hh[h] * hh[h], axis=-1, keepdims=True) + EPS)
        o_ref[:, vs] = (hn * nw_ref[:, vs] * jax.nn.sigmoid(og[:, vs])).astype(o_ref.dtype)


def _mlstm_mixer(p, conv_w, conv_b, gate_b, norm_w, heads, out_dtype=BF16):
    bsz, lp, cols = p.shape
    qk2 = conv_w.shape[-1]
    qk_width = qk2 // 2
    v_width = norm_w.shape[-1]
    kern = functools.partial(_mlstm_kernel, qk_width=qk_width, v_width=v_width, heads=heads)
    full = lambda arr: pl.BlockSpec(arr.shape, lambda b, c: (0,) * arr.ndim)
    small = (conv_w, conv_b, gate_b, norm_w)
    return pl.pallas_call(
        kern,
        grid=(bsz, lp // CHUNK),
        in_specs=[pl.BlockSpec((None, CHUNK, cols), lambda b, c: (b, c, 0))]
        + [full(t) for t in small],
        out_specs=pl.BlockSpec((None, CHUNK, v_width), lambda b, c: (b, c, 0)),
        out_shape=jax.ShapeDtypeStruct((bsz, lp, v_width), out_dtype),
        scratch_shapes=[pltpu.VMEM((heads, qk_width // heads, v_width // heads), F32),
                        pltpu.VMEM((8, qk_width // heads), F32),
                        pltpu.VMEM((8, LANE), F32),
                        pltpu.VMEM((8, qk2), F32)],
        compiler_params=_params(2),
    )(p, *small)


def _out_proj_kernel(h_ref, ya_ref, yb_ref, yc_ref, wa_ref, wb_ref, wc_ref, o_ref):
    acc = jnp.dot(ya_ref[...], wa_ref[...], preferred_element_type=F32)
    acc += jnp.dot(yb_ref[...], wb_ref[...], preferred_element_type=F32)
    acc += jnp.dot(yc_ref[...], wc_ref[...], preferred_element_type=F32)
    o_ref[...] = h_ref[...] + acc


def _out_proj(h, ya, yb, yc, wa, wb, wc, tm):
    m, d = h.shape
    row = lambda arr: pl.BlockSpec((tm, arr.shape[1]), lambda i: (i, 0))
    full = lambda arr: pl.BlockSpec(arr.shape, lambda i: (0, 0))
    return pl.pallas_call(
        _out_proj_kernel,
        grid=(m // tm,),
        in_specs=[row(h), row(ya), row(yb), row(yc), full(wa), full(wb), full(wc)],
        out_specs=pl.BlockSpec((tm, d), lambda i: (i, 0)),
        out_shape=jax.ShapeDtypeStruct((m, d), F32),
        compiler_params=_params(1),
    )(h, ya, yb, yc, wa, wb, wc)


def _ffn_kernel(h_ref, g_ref, w1_ref, w3_ref, w2_ref, gf_ref, o_ref, u_ref, *, final_norm):
    f = pl.program_id(1)

    @pl.when(f == 0)
    def _():
        x = h_ref[...]
        ms = jnp.mean(x * x, axis=-1, keepdims=True)
        u_ref[...] = (x * lax.rsqrt(ms + EPS) * g_ref[...]).astype(BF16)
        o_ref[...] = x

    u = u_ref[...]
    a = jnp.dot(u, w1_ref[...], preferred_element_type=F32)
    b = jnp.dot(u, w3_ref[...], preferred_element_type=F32)
    act = (a * jax.nn.sigmoid(a) * b).astype(BF16)
    o_ref[...] += jnp.dot(act, w2_ref[...], preferred_element_type=F32)

    if final_norm:
        @pl.when(f == pl.num_programs(1) - 1)
        def _():
            y = o_ref[...]
            ms = jnp.mean(y * y, axis=-1, keepdims=True)
            o_ref[...] = y * lax.rsqrt(ms + EPS) * gf_ref[...]


def _ffn(h, g, w1, w3, w2, g_final, tm, tf, final_norm):
    m, d = h.shape
    dff = w1.shape[1]
    return pl.pallas_call(
        functools.partial(_ffn_kernel, final_norm=final_norm),
        grid=(m // tm, dff // tf),
        in_specs=[
            pl.BlockSpec((tm, d), lambda i, f: (i, 0)),
            pl.BlockSpec((1, d), lambda i, f: (0, 0)),
            pl.BlockSpec((d, tf), lambda i, f: (0, f)),
            pl.BlockSpec((d, tf), lambda i, f: (0, f)),
            pl.BlockSpec((tf, d), lambda i, f: (f, 0)),
            pl.BlockSpec((1, d), lambda i, f: (0, 0)),
        ],
        out_specs=pl.BlockSpec((tm, d), lambda i, f: (i, 0)),
        out_shape=jax.ShapeDtypeStruct((m, d), F32),
        scratch_shapes=[pltpu.VMEM((tm, d), BF16)],
        compiler_params=_params(2),
    )(h, g, w1, w3, w2, g_final)


def _pad_groups(w, sizes, axis=-1):
    pieces = []
    off = 0
    for s in sizes:
        piece = lax.slice_in_dim(w, off, off + s, axis=axis)
        pad = _rup(s, LANE) - s
        if pad:
            cfg = [(0, 0)] * w.ndim
            cfg[axis] = (0, pad)
            piece = jnp.pad(piece, cfg)
        pieces.append(piece)
        off += s
    return jnp.concatenate(pieces, axis=axis)


def _tile_rows(m):
    for t in (640, 512, 320, 256, 128, 64, 32, 16, 8):
        if m % t == 0:
            return t
    return m


def _tile_cols(n, cap):
    best = LANE
    for t in range(LANE, cap + 1, LANE):
        if n % t == 0:
            best = t
    return best


def kernel(x, meta_tokens, norm_mix, w_in, rw_mu, rw_w0, rw_w2, rw_a0, rw_a2, rw_g2, rw_kk, rw_ka, rw_rk, rw_ln_w, rw_ln_b, gla_a2, gla_ab, gla_norm, ml_conv_w, ml_conv_b, ml_ib, ml_fb, ml_norm, w_out, norm_ffn, ffn_w1, ffn_w3, ffn_w2, norm_final):
    bsz, seq, d = x.shape
    depth = w_in.shape[0]
    n_meta = meta_tokens.shape[0]
    ltot = n_meta + seq
    lp = _rup(ltot, CHUNK)

    a_width = rw_w0.shape[-1]
    d_rank, a_rank, g_rank = rw_w2.shape[1], rw_a2.shape[1], rw_g2.shape[1]
    a_sizes = [a_width, a_width, a_width, d_rank, a_rank, g_rank]
    b_rank, b_qk = gla_a2.shape[1], gla_a2.shape[2]
    b_heads = b_qk // (gla_norm.shape[-1] // 2)
    b_width = b_heads * gla_norm.shape[-1]
    b_sizes = [b_qk, b_qk, b_width, b_rank, b_width]
    c_heads = ml_ib.shape[-1]
    c_qk = ml_conv_w.shape[-1] // 2
    c_width = ml_norm.shape[-1]
    c_sizes = [c_qk, c_qk, c_width, 2 * c_heads, c_width]
    a_cols, b_cols = sum(a_sizes), sum(b_sizes)

    meta = jnp.broadcast_to(meta_tokens[None].astype(x.dtype), (bsz, n_meta, d))
    h = jnp.concatenate([meta, x, jnp.zeros((bsz, lp - ltot, d), x.dtype)], axis=1)
    h = h.reshape(bsz * lp, d)
    m = bsz * lp
    tm = _tile_rows(m)
    row2 = lambda t: t.reshape(1, -1)

    for l in range(depth):
        w_l = w_in[l]
        wa = _pad_groups(w_l[:, :a_cols], a_sizes).astype(BF16)
        wb = _pad_groups(w_l[:, a_cols:a_cols + b_cols], b_sizes).astype(BF16)
        wc = _pad_groups(w_l[:, a_cols + b_cols:], c_sizes).astype(BF16)
        g_mix = row2(norm_mix[l])
        p_a = _norm_matmul(h, g_mix, wa, tm, _tile_cols(wa.shape[1], 2048))
        p_b = _norm_matmul(h, g_mix, wb, tm, _tile_cols(wb.shape[1], 2048))
        p_c = _norm_matmul(h, g_mix, wc, tm, _tile_cols(wc.shape[1], 2048))

        y_a = _rwkv_mixer(
            p_a.reshape(bsz, lp, -1), _pad_groups(row2(rw_mu[l]), a_sizes),
            row2(rw_w0[l]), rw_w2[l], row2(rw_a0[l]), rw_a2[l], rw_g2[l], row2(rw_kk[l]),
            row2(rw_ka[l]), row2(rw_rk[l]), row2(rw_ln_w[l]), row2(rw_ln_b[l]))
        y_b = _gla_mixer(p_b.reshape(bsz, lp, -1), gla_a2[l], row2(gla_ab[l]),
                         row2(gla_norm[l]))
        gate_b = jnp.pad(jnp.concatenate([ml_ib[l], ml_fb[l]]), (0, LANE - 2 * c_heads))
        y_c = _mlstm_mixer(p_c.reshape(bsz, lp, -1), ml_conv_w[l], row2(ml_conv_b[l]),
                           row2(gate_b), row2(ml_norm[l]), c_heads)

        wo = w_out[l].astype(BF16)
        h = _out_proj(h, y_a.reshape(m, -1), y_b.reshape(m, -1), y_c.reshape(m, -1),
                      wo[:a_width], wo[a_width:a_width + b_width], wo[a_width + b_width:], tm)
        last = l == depth - 1
        h = _ffn(h, row2(norm_ffn[l]), ffn_w1[l].astype(BF16), ffn_w3[l].astype(BF16),
                 ffn_w2[l].astype(BF16), row2(norm_final), tm,
                 _tile_cols(ffn_w1.shape[-1], 512), last)

    return h.reshape(bsz, lp, d)[:, n_meta:ltot]
```

```python
import functools

import jax
import jax.numpy as jnp
from jax import lax
from jax.experimental import pallas as pl
from jax.experimental.pallas import tpu as pltpu

F32 = jnp.float32
BF16 = jnp.bfloat16

EPS = 1e-6
N_META = 16
CHUNK = 64
A_SEQS_PER_STEP = 4
B_SEQS_PER_STEP = 4
C_SEQS_PER_STEP = 1

A_HEAD_DIM = 64
A_GN_EPS = 64e-5
B_DV = 128
B_GATE_NORMALIZER = 16.0
B_SUB = 16
C_DV = 256
C_CONV = 4
C_GATE_CAP = 15.0

LANE = 128
VMEM_LIMIT = 56 * 1024 * 1024

NN = (((1,), (0,)), ((), ()))
NT = (((1,), (1,)), ((), ()))
TN = (((0,), (0,)), ((), ()))

GATE_PASSES = 3
A_PASSES = 1
A_INV_PASSES = 1
B_PASSES = 1
C_PASSES = 1


def _rup(n, m):
    return -(-n // m) * m


def _split(x, parts):
    out = []
    rem = x
    for i in range(parts):
        hi = rem.astype(BF16)
        out.append(hi)
        if i + 1 < parts:
            rem = rem - hi.astype(F32)
    return out


def _mm_parts(a_parts, b_parts, dims):
    dg = lambda a, b: lax.dot_general(a, b, dims, preferred_element_type=F32)
    acc = dg(a_parts[0], b_parts[0])
    if len(a_parts) > 1:
        acc = acc + dg(a_parts[0], b_parts[1]) + dg(a_parts[1], b_parts[0])
    return acc


def _mm(a, b, dims=NN, passes=1):
    n = 2 if passes == 3 else 1
    return _mm_parts(_split(a, n), _split(b, n), dims)


def _cumsum_rows(x):
    tri = _tril_mask(x.shape[0]).astype(BF16)
    hi, mid, lo = (jnp.dot(tri, part, preferred_element_type=F32) for part in _split(x, 3))
    return hi + (mid + lo)


def _softplus(z):
    return jnp.maximum(z, 0.0) + jnp.log1p(jnp.exp(-jnp.abs(z)))


def _log_sigmoid(z):
    return -_softplus(-z)


def _tril_mask(n, strict=False):
    r = lax.broadcasted_iota(jnp.int32, (n, n), 0)
    c = lax.broadcasted_iota(jnp.int32, (n, n), 1)
    return (r > c) if strict else (r >= c)


def _params(n_axes):
    return pltpu.CompilerParams(
        dimension_semantics=("arbitrary",) * n_axes, vmem_limit_bytes=VMEM_LIMIT)


def _norm_matmul_kernel(h_ref, g_ref, w_ref, o_ref, u_ref):
    @pl.when(pl.program_id(1) == 0)
    def _():
        x = h_ref[...]
        ms = jnp.mean(x * x, axis=-1, keepdims=True)
        u_ref[...] = (x * lax.rsqrt(ms + EPS) * g_ref[...]).astype(BF16)

    o_ref[...] = jnp.dot(u_ref[...], w_ref[...], preferred_element_type=F32)


def _norm_matmul(h, g, w, tm, tn):
    m, d = h.shape
    n = w.shape[1]
    return pl.pallas_call(
        _norm_matmul_kernel,
        grid=(m // tm, n // tn),
        in_specs=[
            pl.BlockSpec((tm, d), lambda i, j: (i, 0)),
            pl.BlockSpec((1, d), lambda i, j: (0, 0)),
            pl.BlockSpec((d, tn), lambda i, j: (0, j)),
        ],
        out_specs=pl.BlockSpec((tm, tn), lambda i, j: (i, j)),
        out_shape=jax.ShapeDtypeStruct((m, n), F32),
        scratch_shapes=[pltpu.VMEM((tm, d), BF16)],
        compiler_params=_params(2),
    )(h, g, w)


def _seqs_per_step(bsz, want):
    return want if bsz % want == 0 else 1


def _round_robin(gens):
    gens = list(gens)
    while gens:
        alive = []
        for gen in gens:
            try:
                next(gen)
                alive.append(gen)
            except StopIteration:
                pass
        gens = alive


def _rwkv_kernel(p_ref, *rest, width, d_rank, a_rank, g_rank):
    s_ref, prev_ref = rest[-2:]

    @pl.when(pl.program_id(1) == 0)
    def _():
        s_ref[...] = jnp.zeros_like(s_ref)
        prev_ref[...] = jnp.zeros_like(prev_ref)

    _round_robin(_rwkv_chunk(i, p_ref, *rest, width=width, d_rank=d_rank, a_rank=a_rank,
                             g_rank=g_rank) for i in range(p_ref.shape[0]))


def _rwkv_chunk(i, p_ref, mu_ref, w0_ref, w2_ref, a0_ref, a2_ref, g2_ref, kk_ref, ka_ref,
                rk_ref, lnw_ref, lnb_ref, o_ref, s_ref, prev_ref, *, width, d_rank,
                a_rank, g_rank):
    c = CHUNK
    n = A_HEAD_DIM
    heads = width // n

    p = p_ref[i]
    row = lax.broadcasted_iota(jnp.int32, p.shape, 0)
    shifted = jnp.where(row == 0, prev_ref[i], pltpu.roll(p, 1, axis=0))
    prev_ref[i] = p[c - 1:c, :]
    x = p + mu_ref[...] * (shifted - p)

    off_w = 3 * width
    off_a = off_w + _rup(d_rank, LANE)
    off_g = off_a + _rup(a_rank, LANE)
    r = x[:, 0:width]
    k = x[:, width:2 * width]
    v = x[:, 2 * width:3 * width]
    xw = x[:, off_w:off_w + d_rank]
    xa = x[:, off_a:off_a + a_rank]
    xg = x[:, off_g:off_g + g_rank]

    w_log = -_softplus(-(w0_ref[...] + _mm(jnp.tanh(xw), w2_ref[...], NN, GATE_PASSES))) - 0.5
    ld = -jnp.exp(w_log)
    a = jax.nn.sigmoid(a0_ref[...] + _mm(xa, a2_ref[...], NN, GATE_PASSES))
    g = _mm(jax.nn.sigmoid(xg), g2_ref[...], NN, GATE_PASSES)
    kk = k * kk_ref[...]
    k2 = k * (1.0 + (a - 1.0) * ka_ref[...])

    yield
    cum = _cumsum_rows(ld)
    e_pos = jnp.exp(cum)
    e_neg = jnp.exp(-cum)
    e_prev = jnp.exp(cum - ld)
    e_end = jnp.exp(cum[c - 1:c, :] - cum)

    low2 = (lax.broadcasted_iota(jnp.int32, (c, 2 * c), 0)
            >= lax.broadcasted_iota(jnp.int32, (c, 2 * c), 1) % c)
    slow = _tril_mask(c, strict=True)
    eye = (lax.broadcasted_iota(jnp.int32, (c, c), 0)
           == lax.broadcasted_iota(jnp.int32, (c, c), 1)).astype(F32)
    hs = range(heads)
    sls = [slice(h * n, (h + 1) * n) for h in hs]
    np_ = 2 if A_PASSES == 3 else 1

    same_head = (lax.broadcasted_iota(jnp.int32, (width, width), 0) // n
                 == lax.broadcasted_iota(jnp.int32, (width, width), 1) // n).astype(BF16)
    yield
    ssq = None
    for part in reversed(_split(kk * kk, 3)):
        t = jnp.dot(part, same_head, preferred_element_type=F32)
        ssq = t if ssq is None else ssq + t
    kk = kk / jnp.maximum(jnp.sqrt(ssq), 1e-12)
    b = kk * a
    lhs_f = jnp.concatenate([-kk * e_prev, r * e_pos], axis=0)
    rhs_f = jnp.concatenate([b * e_neg, k2 * e_neg], axis=0)
    end_f = jnp.concatenate([b * e_end, k2 * e_end], axis=0)
    lhs = [_split(lhs_f[:, sl], np_) for sl in sls]
    rhs = [_split(rhs_f[:, sl], np_) for sl in sls]

    yield
    amat = [_mm_parts(lhs[h], rhs[h], NT) for h in hs]
    s0 = [s_ref[i, h] for h in hs]
    ls = [_mm_parts(lhs[h], _split(s0[h], np_), NT) for h in hs]

    yield
    a_ab = [jnp.where(slow, amat[h][0:c, 0:c], 0.0) for h in hs]
    rhs_u = [ls[h][0:c] + _mm(jnp.where(slow, amat[h][0:c, c:2 * c], 0.0), v[:, sls[h]],
                              NN, A_PASSES) for h in hs]

    yield
    tinv = [eye + a_ab[h] for h in hs]
    pw = [_mm(a_ab[h], a_ab[h], NN, A_INV_PASSES) for h in hs]
    for _ in range((c - 1).bit_length() - 2):
        yield
        res = [_mm(jnp.concatenate([pw[h], tinv[h]], axis=0), pw[h], NN, A_INV_PASSES)
               for h in hs]
        pw = [res[h][0:c] for h in hs]
        tinv = [tinv[h] + res[h][c:2 * c] for h in hs]
    yield
    tinv = [tinv[h] + _mm(tinv[h], pw[h], NN, A_INV_PASSES) for h in hs]
    yield
    u = [_mm(tinv[h], rhs_u[h], NN, A_INV_PASSES) for h in hs]
    yield
    uv = [_split(jnp.concatenate([u[h], v[:, sls[h]]], axis=0), np_) for h in hs]
    y = [ls[h][c:2 * c] + _mm_parts(_split(jnp.where(low2, amat[h][c:2 * c, :], 0.0), np_),
                                    uv[h], NN) for h in hs]
    for h in hs:
        sl = sls[h]
        s_ref[i, h] = s0[h] * e_pos[c - 1:c, sl] + _mm_parts(uv[h], _split(end_f[:, sl], np_), TN)

    yield
    for h in hs:
        sl = sls[h]
        mean = jnp.mean(y[h], axis=-1, keepdims=True)
        var = jnp.mean(jnp.square(y[h] - mean), axis=-1, keepdims=True)
        yn = (y[h] - mean) * lax.rsqrt(var + A_GN_EPS)
        yn = yn * lnw_ref[:, sl] + lnb_ref[:, sl]
        bonus = jnp.sum(r[:, sl] * k2[:, sl] * rk_ref[:, sl], axis=-1, keepdims=True) * v[:, sl]
        o_ref[i, :, sl] = ((yn + bonus) * g[:, sl]).astype(o_ref.dtype)


def _rwkv_mixer(p, mu, w0, w2, a0, a2, g2, k_k, k_a, r_k, ln_w, ln_b, out_dtype=BF16):
    bsz, lp, cols = p.shape
    nb = _seqs_per_step(bsz, A_SEQS_PER_STEP)
    width = w0.shape[-1]
    heads = width // A_HEAD_DIM
    kern = functools.partial(_rwkv_kernel, width=width, d_rank=w2.shape[0],
                             a_rank=a2.shape[0], g_rank=g2.shape[0])
    full = lambda arr: pl.BlockSpec(arr.shape, lambda b, c: (0,) * arr.ndim)
    small = (mu, w0, w2, a0, a2, g2, k_k, k_a, r_k, ln_w, ln_b)
    return pl.pallas_call(
        kern,
        grid=(bsz // nb, lp // CHUNK),
        in_specs=[pl.BlockSpec((nb, CHUNK, cols), lambda b, c: (b, c, 0))]
        + [full(t) for t in small],
        out_specs=pl.BlockSpec((nb, CHUNK, width), lambda b, c: (b, c, 0)),
        out_shape=jax.ShapeDtypeStruct((bsz, lp, width), out_dtype),
        scratch_shapes=[pltpu.VMEM((nb, heads, A_HEAD_DIM, A_HEAD_DIM), F32),
                        pltpu.VMEM((nb, 1, cols), F32)],
        compiler_params=_params(2),
    )(p, *small)


def _gla_kernel(p_ref, *rest, qk_width, v_width, rank):
    s_ref = rest[-1]

    @pl.when(pl.program_id(1) == 0)
    def _():
        s_ref[...] = jnp.zeros_like(s_ref)

    _round_robin(_gla_chunk(i, p_ref, *rest, qk_width=qk_width, v_width=v_width, rank=rank)
                 for i in range(p_ref.shape[0]))


def _gla_chunk(i, p_ref, a2_ref, ab_ref, nw_ref, o_ref, s_ref, *, qk_width, v_width, rank):
    c = CHUNK
    dv = B_DV
    heads = v_width // dv
    dk = qk_width // heads

    off_v = 2 * qk_width
    off_a = off_v + v_width
    off_g = off_a + _rup(rank, LANE)
    p = p_ref[i]
    q = p[:, 0:qk_width] * (dk ** -0.5)
    k = p[:, qk_width:off_v]
    v = p[:, off_v:off_a]
    xa = p[:, off_a:off_a + rank]
    gate = p[:, off_g:off_g + v_width]

    lg = _log_sigmoid(_mm(xa, a2_ref[...], NN, GATE_PASSES) + ab_ref[...]) / B_GATE_NORMALIZER
    yield
    bcum = _cumsum_rows(lg)
    low = _tril_mask(c)
    b_last = bcum[c - 1:c, :]
    q_in = q * jnp.exp(bcum)
    k_end = k * jnp.exp(b_last - bcum)
    e_last = jnp.exp(b_last)

    qf, kf = [], []
    for i0 in range(0, c, B_SUB):
        i1 = i0 + B_SUB
        ref_pt = bcum[i1 - 1:i1, :]
        qf.append(q[i0:i1] * jnp.exp(bcum[i0:i1] - ref_pt))
        kf.append(k[0:i1] * jnp.exp(ref_pt - bcum[0:i1]))

    hs = range(heads)
    sls = [slice(h * dk, (h + 1) * dk) for h in hs]
    vss = [slice(h * dv, (h + 1) * dv) for h in hs]
    yield
    st = [s_ref[i, h] for h in hs]
    o = [_mm(q_in[:, sls[h]], st[h], NT, B_PASSES) for h in hs]
    starts = list(range(0, c, B_SUB))
    sc = [[_mm(qf[bi][:, sls[h]], kf[bi][:, sls[h]], NT, B_PASSES)
           for bi in range(len(starts))] for h in hs]
    yield
    sc = [[jnp.where(low[i0:i0 + B_SUB, 0:i0 + B_SUB], sc[h][bi], 0.0)
           for bi, i0 in enumerate(starts)] for h in hs]
    for h in hs:
        blocks = [_mm(sc[h][bi], v[0:i0 + B_SUB, vss[h]], NN, B_PASSES)
                  for bi, i0 in enumerate(starts)]
        o[h] = o[h] + jnp.concatenate(blocks, axis=0)
    for h in hs:
        s_ref[i, h] = st[h] * e_last[:, sls[h]] + _mm(v[:, vss[h]], k_end[:, sls[h]], TN, B_PASSES)
    yield
    for h in hs:
        vs = vss[h]
        o_h = o[h] * lax.rsqrt(jnp.mean(o[h] * o[h], axis=-1, keepdims=True) + EPS) * nw_ref[...]
        g_h = gate[:, vs]
        o_ref[i, :, vs] = (o_h * (g_h * jax.nn.sigmoid(g_h))).astype(o_ref.dtype)


def _gla_mixer(p, a2, ab, norm_w, out_dtype=BF16):
    bsz, lp, cols = p.shape
    rank, qk_width = a2.shape
    dv = norm_w.shape[-1]
    heads = qk_width // (dv // 2)
    v_width = heads * dv
    nb = _seqs_per_step(bsz, B_SEQS_PER_STEP)
    kern = functools.partial(_gla_kernel, qk_width=qk_width, v_width=v_width, rank=rank)
    full = lambda arr: pl.BlockSpec(arr.shape, lambda b, c: (0,) * arr.ndim)
    small = (a2, ab, norm_w)
    return pl.pallas_call(
        kern,
        grid=(bsz // nb, lp // CHUNK),
        in_specs=[pl.BlockSpec((nb, CHUNK, cols), lambda b, c: (b, c, 0))]
        + [full(t) for t in small],
        out_specs=pl.BlockSpec((nb, CHUNK, v_width), lambda b, c: (b, c, 0)),
        out_shape=jax.ShapeDtypeStruct((bsz, lp, v_width), out_dtype),
        scratch_shapes=[pltpu.VMEM((nb, heads, dv, qk_width // heads), F32)],
        compiler_params=_params(2),
    )(p, *small)


def _mlstm_kernel(p_ref, *rest, qk_width, v_width, heads):
    @pl.when(pl.program_id(1) == 0)
    def _():
        for ref in rest[-4:]:
            ref[...] = jnp.zeros_like(ref)

    _round_robin(_mlstm_chunk(i, p_ref, *rest, qk_width=qk_width, v_width=v_width,
                              heads=heads) for i in range(p_ref.shape[0]))


def _mlstm_chunk(i, p_ref, cw_ref, cb_ref, gb_ref, nw_ref, o_ref, cm_ref, n_ref, m_ref,
                 hist_ref, *, qk_width, v_width, heads):
    c = CHUNK
    dqk = qk_width // heads
    dv = v_width // heads

    off_v = 2 * qk_width
    off_g = off_v + v_width
    off_o = off_g + LANE
    p = p_ref[i]
    qk_raw = p[:, 0:off_v]
    v = p[:, off_v:off_g]
    gates = p[:, off_g:off_o]
    og = p[:, off_o:off_o + v_width]

    ext = jnp.concatenate([hist_ref[i], qk_raw], axis=0)
    hist_ref[i] = qk_raw[c - 8:c, :]
    conv = cb_ref[...] + ext * cw_ref[C_CONV - 1:C_CONV, :]
    for d in range(1, C_CONV):
        conv = conv + pltpu.roll(ext, d, axis=0) * cw_ref[C_CONV - 1 - d:C_CONV - d, :]
    conv = conv[8:, :]
    qk = conv * jax.nn.sigmoid(conv)
    q = qk[:, 0:qk_width]
    k = qk[:, qk_width:off_v] * (dqk ** -0.5)

    capped = C_GATE_CAP * jnp.tanh((gates + gb_ref[...]) / C_GATE_CAP)
    lf = _log_sigmoid(capped)
    yield
    bcol = _cumsum_rows(lf)
    brow = bcol.T
    irow = capped.T
    low = _tril_mask(c)
    hs = range(heads)
    qss = [slice(h * dqk, (h + 1) * dqk) for h in hs]
    vss = [slice(h * dv, (h + 1) * dv) for h in hs]
    np_ = 2 if C_PASSES == 3 else 1

    q_p = [_split(q[:, qss[h]], np_) for h in hs]
    k_p = [_split(k[:, qss[h]], np_) for h in hs]
    yield
    cm = [cm_ref[i, h] for h in hs]
    qk_s = [_mm_parts(q_p[h], k_p[h], NT) for h in hs]
    q_cm = [_mm_parts(q_p[h], _split(cm[h], np_), NN) for h in hs]

    yield
    bc = [bcol[:, heads + h:heads + h + 1] for h in hs]
    ic = [capped[:, h:h + 1] for h in hs]
    m_prev = [m_ref[i, h:h + 1, 0:1] for h in hs]
    n_row = [n_ref[i, h:h + 1, :] for h in hs]
    log_w = [jnp.where(low, bc[h] - brow[heads + h:heads + h + 1, :] + irow[h:h + 1, :],
                       -jnp.inf) for h in hs]
    log_prev = [bc[h] + m_prev[h] for h in hs]
    m_t = [jnp.maximum(log_prev[h], jnp.max(log_w[h], axis=-1, keepdims=True)) for h in hs]
    w_prev = [jnp.exp(log_prev[h] - m_t[h]) for h in hs]
    s = [qk_s[h] * jnp.exp(log_w[h] - m_t[h]) for h in hs]
    yield
    num = [w_prev[h] * q_cm[h] + _mm(s[h], v[:, vss[h]], NN, C_PASSES) for h in hs]
    den = [w_prev[h] * jnp.sum(q[:, qss[h]] * n_row[h], axis=-1, keepdims=True)
           + jnp.sum(s[h], axis=-1, keepdims=True) for h in hs]
    hh = [num[h] / jnp.maximum(jnp.abs(den[h]), jnp.exp(-m_t[h])) for h in hs]

    yield
    m_new = [m_t[h][c - 1:c, :] for h in hs]
    b_last = [bc[h][c - 1:c, :] for h in hs]
    kw = [k[:, qss[h]] * jnp.exp(b_last[h] - bc[h] + ic[h] - m_new[h]) for h in hs]
    f_end = [jnp.exp(b_last[h] + m_prev[h] - m_new[h]) for h in hs]
    for h in hs:
        cm_ref[i, h] = f_end[h] * cm[h] + _mm(kw[h], v[:, vss[h]], TN, C_PASSES)
        n_ref[i, h:h + 1, :] = f_end[h] * n_row[h] + jnp.sum(kw[h], axis=0, keepdims=True)
        m_ref[i, h:h + 1, :] = jnp.broadcast_to(m_new[h], (1, m_ref.shape[-1]))

    yield
    for h in hs:
        vs = vss[h]
        hn = hh[h] * lax.rsqrt(jnp.mean(hh[h] * hh[h], axis=-1, keepdims=True) + EPS)
        o_ref[i, :, vs] = (hn * nw_ref[:, vs] * jax.nn.sigmoid(og[:, vs])).astype(o_ref.dtype)


def _mlstm_mixer(p, conv_w, conv_b, gate_b, norm_w, heads, out_dtype=BF16):
    bsz, lp, cols = p.shape
    qk2 = conv_w.shape[-1]
    qk_width = qk2 // 2
    v_width = norm_w.shape[-1]
    nb = _seqs_per_step(bsz, C_SEQS_PER_STEP)
    kern = functools.partial(_mlstm_kernel, qk_width=qk_width, v_width=v_width, heads=heads)
    full = lambda arr: pl.BlockSpec(arr.shape, lambda b, c: (0,) * arr.ndim)
    small = (conv_w, conv_b, gate_b, norm_w)
    return pl.pallas_call(
        kern,
        grid=(bsz // nb, lp // CHUNK),
        in_specs=[pl.BlockSpec((nb, CHUNK, cols), lambda b, c: (b, c, 0))]
        + [full(t) for t in small],
        out_specs=pl.BlockSpec((nb, CHUNK, v_width), lambda b, c: (b, c, 0)),
        out_shape=jax.ShapeDtypeStruct((bsz, lp, v_width), out_dtype),
        scratch_shapes=[pltpu.VMEM((nb, heads, qk_width // heads, v_width // heads), F32),
                        pltpu.VMEM((nb, 8, qk_width // heads), F32),
                        pltpu.VMEM((nb, 8, LANE), F32),
                        pltpu.VMEM((nb, 8, qk2), F32)],
        compiler_params=_params(2),
    )(p, *small)


def _out_proj_kernel(h_ref, ya_ref, yb_ref, yc_ref, wa_ref, wb_ref, wc_ref, o_ref):
    acc = jnp.dot(ya_ref[...], wa_ref[...], preferred_element_type=F32)
    acc += jnp.dot(yb_ref[...], wb_ref[...], preferred_element_type=F32)
    acc += jnp.dot(yc_ref[...], wc_ref[...], preferred_element_type=F32)
    o_ref[...] = h_ref[...] + acc


def _out_proj(h, ya, yb, yc, wa, wb, wc, tm):
    m, d = h.shape
    row = lambda arr: pl.BlockSpec((tm, arr.shape[1]), lambda i: (i, 0))
    full = lambda arr: pl.BlockSpec(arr.shape, lambda i: (0, 0))
    return pl.pallas_call(
        _out_proj_kernel,
        grid=(m // tm,),
        in_specs=[row(h), row(ya), row(yb), row(yc), full(wa), full(wb), full(wc)],
        out_specs=pl.BlockSpec((tm, d), lambda i: (i, 0)),
        out_shape=jax.ShapeDtypeStruct((m, d), F32),
        compiler_params=_params(1),
    )(h, ya, yb, yc, wa, wb, wc)


def _ffn_kernel(h_ref, g_ref, w1_ref, w3_ref, w2_ref, gf_ref, o_ref, u_ref, *, final_norm):
    f = pl.program_id(1)

    @pl.when(f == 0)
    def _():
        x = h_ref[...]
        ms = jnp.mean(x * x, axis=-1, keepdims=True)
        u_ref[...] = (x * lax.rsqrt(ms + EPS) * g_ref[...]).astype(BF16)
        o_ref[...] = x

    u = u_ref[...]
    a = jnp.dot(u, w1_ref[...], preferred_element_type=F32)
    b = jnp.dot(u, w3_ref[...], preferred_element_type=F32)
    act = (a * jax.nn.sigmoid(a) * b).astype(BF16)
    o_ref[...] += jnp.dot(act, w2_ref[...], preferred_element_type=F32)

    if final_norm:
        @pl.when(f == pl.num_programs(1) - 1)
        def _():
            y = o_ref[...]
            ms = jnp.mean(y * y, axis=-1, keepdims=True)
            o_ref[...] = y * lax.rsqrt(ms + EPS) * gf_ref[...]


def _ffn(h, g, w1, w3, w2, g_final, tm, tf, final_norm):
    m, d = h.shape
    dff = w1.shape[1]
    return pl.pallas_call(
        functools.partial(_ffn_kernel, final_norm=final_norm),
        grid=(m // tm, dff // tf),
        in_specs=[
            pl.BlockSpec((tm, d), lambda i, f: (i, 0)),
            pl.BlockSpec((1, d), lambda i, f: (0, 0)),
            pl.BlockSpec((d, tf), lambda i, f: (0, f)),
            pl.BlockSpec((d, tf), lambda i, f: (0, f)),
            pl.BlockSpec((tf, d), lambda i, f: (f, 0)),
            pl.BlockSpec((1, d), lambda i, f: (0, 0)),
        ],
        out_specs=pl.BlockSpec((tm, d), lambda i, f: (i, 0)),
        out_shape=jax.ShapeDtypeStruct((m, d), F32),
        scratch_shapes=[pltpu.VMEM((tm, d), BF16)],
        compiler_params=_params(2),
    )(h, g, w1, w3, w2, g_final)


def _pad_groups(w, sizes, axis=-1):
    pieces = []
    off = 0
    for s in sizes:
        piece = lax.slice_in_dim(w, off, off + s, axis=axis)
        pad = _rup(s, LANE) - s
        if pad:
            cfg = [(0, 0)] * w.ndim
            cfg[axis] = (0, pad)
            piece = jnp.pad(piece, cfg)
        pieces.append(piece)
        off += s
    return jnp.concatenate(pieces, axis=axis)


def _tile_rows(m):
    for t in (640, 512, 320, 256, 128, 64, 32, 16, 8):
        if m % t == 0:
            return t
    return m


def _proj_tiles(m, d, n):
    budget = VMEM_LIMIT * 3 // 4
    col_tiles = [t for t in range(n, 0, -LANE) if n % t == 0]
    row_tiles = [t for t in (640, 320, 256, 128, 64, 32, 16, 8) if m % t == 0] or [m]
    for tn in col_tiles:
        for tm in row_tiles:
            need = 2 * tm * d * 4 + tm * d * 2 + 2 * d * tn * 2 + 2 * tm * tn * 4
            if need <= budget:
                return tm, tn
    return row_tiles[-1], col_tiles[-1]


def _tile_cols(n, cap):
    best = LANE
    for t in range(LANE, cap + 1, LANE):
        if n % t == 0:
            best = t
    return best


def kernel(x, meta_tokens, norm_mix, w_in, rw_mu, rw_w0, rw_w2, rw_a0, rw_a2, rw_g2, rw_kk, rw_ka, rw_rk, rw_ln_w, rw_ln_b, gla_a2, gla_ab, gla_norm, ml_conv_w, ml_conv_b, ml_ib, ml_fb, ml_norm, w_out, norm_ffn, ffn_w1, ffn_w3, ffn_w2, norm_final):
    bsz, seq, d = x.shape
    depth = w_in.shape[0]
    n_meta = meta_tokens.shape[0]
    ltot = n_meta + seq
    lp = _rup(ltot, CHUNK)

    a_width = rw_w0.shape[-1]
    d_rank, a_rank, g_rank = rw_w2.shape[1], rw_a2.shape[1], rw_g2.shape[1]
    a_sizes = [a_width, a_width, a_width, d_rank, a_rank, g_rank]
    b_rank, b_qk = gla_a2.shape[1], gla_a2.shape[2]
    b_heads = b_qk // (gla_norm.shape[-1] // 2)
    b_width = b_heads * gla_norm.shape[-1]
    b_sizes = [b_qk, b_qk, b_width, b_rank, b_width]
    c_heads = ml_ib.shape[-1]
    c_qk = ml_conv_w.shape[-1] // 2
    c_width = ml_norm.shape[-1]
    c_sizes = [c_qk, c_qk, c_width, 2 * c_heads, c_width]
    a_cols, b_cols = sum(a_sizes), sum(b_sizes)

    meta = jnp.broadcast_to(meta_tokens[None].astype(x.dtype), (bsz, n_meta, d))
    h = jnp.concatenate([meta, x, jnp.zeros((bsz, lp - ltot, d), x.dtype)], axis=1)
    h = h.reshape(bsz * lp, d)
    m = bsz * lp
    tm = _tile_rows(m)
    row2 = lambda t: t.reshape(1, -1)

    for l in range(depth):
        w_l = w_in[l]
        wa = _pad_groups(w_l[:, :a_cols], a_sizes).astype(BF16)
        wb = _pad_groups(w_l[:, a_cols:a_cols + b_cols], b_sizes).astype(BF16)
        wc = _pad_groups(w_l[:, a_cols + b_cols:], c_sizes).astype(BF16)
        g_mix = row2(norm_mix[l])
        p_a = _norm_matmul(h, g_mix, wa, *_proj_tiles(m, d, wa.shape[1]))
        p_b = _norm_matmul(h, g_mix, wb, *_proj_tiles(m, d, wb.shape[1]))
        p_c = _norm_matmul(h, g_mix, wc, *_proj_tiles(m, d, wc.shape[1]))

        y_a = _rwkv_mixer(
            p_a.reshape(bsz, lp, -1), _pad_groups(row2(rw_mu[l]), a_sizes),
            row2(rw_w0[l]), rw_w2[l], row2(rw_a0[l]), rw_a2[l], rw_g2[l], row2(rw_kk[l]),
            row2(rw_ka[l]), row2(rw_rk[l]), row2(rw_ln_w[l]), row2(rw_ln_b[l]))
        y_b = _gla_mixer(p_b.reshape(bsz, lp, -1), gla_a2[l], row2(gla_ab[l]),
                         row2(gla_norm[l]))
        gate_b = jnp.pad(jnp.concatenate([ml_ib[l], ml_fb[l]]), (0, LANE - 2 * c_heads))
        y_c = _mlstm_mixer(p_c.reshape(bsz, lp, -1), ml_conv_w[l], row2(ml_conv_b[l]),
                           row2(gate_b), row2(ml_norm[l]), c_heads)

        wo = w_out[l].astype(BF16)
        h = _out_proj(h, y_a.reshape(m, -1), y_b.reshape(m, -1), y_c.reshape(m, -1),
                      wo[:a_width], wo[a_width:a_width + b_width], wo[a_width + b_width:], tm)
        last = l == depth - 1
        h = _ffn(h, row2(norm_ffn[l]), ffn_w1[l].astype(BF16), ffn_w3[l].astype(BF16),
                 ffn_w2[l].astype(BF16), row2(norm_final), tm,
                 _tile_cols(ffn_w1.shape[-1], 512), last)

    return h.reshape(bsz, lp, d)[:, n_meta:ltot]
```

```python
import functools

import jax
import jax.numpy as jnp
from jax import lax
from jax.experimental import pallas as pl
from jax.experimental.pallas import tpu as pltpu

F32 = jnp.float32
BF16 = jnp.bfloat16

EPS = 1e-6
N_META = 16
CHUNK = 64
A_SEQS_PER_STEP = 4
B_SEQS_PER_STEP = 4
C_SEQS_PER_STEP = 1

A_HEAD_DIM = 64
A_GN_EPS = 64e-5
B_DV = 128
B_GATE_NORMALIZER = 16.0
B_SUB = 16
C_DV = 256
C_CONV = 4
C_GATE_CAP = 15.0

LANE = 128
VMEM_LIMIT = 56 * 1024 * 1024

NN = (((1,), (0,)), ((), ()))
NT = (((1,), (1,)), ((), ()))
TN = (((0,), (0,)), ((), ()))

GATE_PASSES = 3
A_PASSES = 1
A_INV_PASSES = 1
B_PASSES = 1
C_PASSES = 1


def _rup(n, m):
    return -(-n // m) * m


def _split(x, parts):
    out = []
    rem = x
    for i in range(parts):
        hi = rem.astype(BF16)
        out.append(hi)
        if i + 1 < parts:
            rem = rem - hi.astype(F32)
    return out


def _mm_parts(a_parts, b_parts, dims):
    dg = lambda a, b: lax.dot_general(a, b, dims, preferred_element_type=F32)
    acc = dg(a_parts[0], b_parts[0])
    if len(a_parts) > 1:
        acc = acc + dg(a_parts[0], b_parts[1]) + dg(a_parts[1], b_parts[0])
    return acc


def _mm(a, b, dims=NN, passes=1):
    n = 2 if passes == 3 else 1
    return _mm_parts(_split(a, n), _split(b, n), dims)


def _cumsum_rows(x):
    tri = _tril_mask(x.shape[0]).astype(BF16)
    hi, mid, lo = (jnp.dot(tri, part, preferred_element_type=F32) for part in _split(x, 3))
    return hi + (mid + lo)


def _softplus(z):
    return jnp.maximum(z, 0.0) + jnp.log1p(jnp.exp(-jnp.abs(z)))


def _log_sigmoid(z):
    return -_softplus(-z)


def _tril_mask(n, strict=False):
    r = lax.broadcasted_iota(jnp.int32, (n, n), 0)
    c = lax.broadcasted_iota(jnp.int32, (n, n), 1)
    return (r > c) if strict else (r >= c)


def _params(n_axes):
    return pltpu.CompilerParams(
        dimension_semantics=("arbitrary",) * n_axes, vmem_limit_bytes=VMEM_LIMIT)


def _norm_matmul_kernel(h_ref, g_ref, w_ref, o_ref, u_ref):
    @pl.when(pl.program_id(1) == 0)
    def _():
        x = h_ref[...]
        ms = jnp.mean(x * x, axis=-1, keepdims=True)
        u_ref[...] = (x * lax.rsqrt(ms + EPS) * g_ref[...]).astype(BF16)

    o_ref[...] = jnp.dot(u_ref[...], w_ref[...], preferred_element_type=F32)


def _norm_matmul(h, g, w, tm, tn):
    m, d = h.shape
    n = w.shape[1]
    return pl.pallas_call(
        _norm_matmul_kernel,
        grid=(m // tm, n // tn),
        in_specs=[
            pl.BlockSpec((tm, d), lambda i, j: (i, 0)),
            pl.BlockSpec((1, d), lambda i, j: (0, 0)),
            pl.BlockSpec((d, tn), lambda i, j: (0, j)),
        ],
        out_specs=[pl.BlockSpec((tm, tn), lambda i, j: (i, j)),
                   pl.BlockSpec((tm, d), lambda i, j: (i, 0))],
        out_shape=[jax.ShapeDtypeStruct((m, n), F32), jax.ShapeDtypeStruct((m, d), BF16)],
        compiler_params=_params(2),
    )(h, g, w)


def _matmul_kernel(u_ref, w_ref, o_ref):
    o_ref[...] = jnp.dot(u_ref[...], w_ref[...], preferred_element_type=F32)


def _matmul(u, w, tm, tn):
    m, d = u.shape
    n = w.shape[1]
    return pl.pallas_call(
        _matmul_kernel,
        grid=(m // tm, n // tn),
        in_specs=[pl.BlockSpec((tm, d), lambda i, j: (i, 0)),
                  pl.BlockSpec((d, tn), lambda i, j: (0, j))],
        out_specs=pl.BlockSpec((tm, tn), lambda i, j: (i, j)),
        out_shape=jax.ShapeDtypeStruct((m, n), F32),
        compiler_params=_params(2),
    )(u, w)


def _seqs_per_step(bsz, want):
    return want if bsz % want == 0 else 1


def _round_robin(gens):
    gens = list(gens)
    while gens:
        alive = []
        for gen in gens:
            try:
                next(gen)
                alive.append(gen)
            except StopIteration:
                pass
        gens = alive


def _rwkv_kernel(p_ref, *rest, width, d_rank, a_rank, g_rank):
    s_ref, prev_ref = rest[-2:]

    @pl.when(pl.program_id(1) == 0)
    def _():
        s_ref[...] = jnp.zeros_like(s_ref)
        prev_ref[...] = jnp.zeros_like(prev_ref)

    _round_robin(_rwkv_chunk(i, p_ref, *rest, width=width, d_rank=d_rank, a_rank=a_rank,
                             g_rank=g_rank) for i in range(p_ref.shape[0]))


def _rwkv_chunk(i, p_ref, mu_ref, w0_ref, w2_ref, a0_ref, a2_ref, g2_ref, kk_ref, ka_ref,
                rk_ref, lnw_ref, lnb_ref, o_ref, s_ref, prev_ref, *, width, d_rank,
                a_rank, g_rank):
    c = CHUNK
    n = A_HEAD_DIM
    heads = width // n

    p = p_ref[i]
    row = lax.broadcasted_iota(jnp.int32, p.shape, 0)
    shifted = jnp.where(row == 0, prev_ref[i], pltpu.roll(p, 1, axis=0))
    prev_ref[i] = p[c - 1:c, :]
    x = p + mu_ref[...] * (shifted - p)

    off_w = 3 * width
    off_a = off_w + _rup(d_rank, LANE)
    off_g = off_a + _rup(a_rank, LANE)
    r = x[:, 0:width]
    k = x[:, width:2 * width]
    v = x[:, 2 * width:3 * width]
    xw = x[:, off_w:off_w + d_rank]
    xa = x[:, off_a:off_a + a_rank]
    xg = x[:, off_g:off_g + g_rank]

    w_log = -_softplus(-(w0_ref[...] + _mm(jnp.tanh(xw), w2_ref[...], NN, GATE_PASSES))) - 0.5
    ld = -jnp.exp(w_log)
    a = jax.nn.sigmoid(a0_ref[...] + _mm(xa, a2_ref[...], NN, GATE_PASSES))
    g = _mm(jax.nn.sigmoid(xg), g2_ref[...], NN, GATE_PASSES)
    kk = k * kk_ref[...]
    k2 = k * (1.0 + (a - 1.0) * ka_ref[...])

    yield
    cum = _cumsum_rows(ld)
    e_pos = jnp.exp(cum)
    e_neg = jnp.exp(-cum)
    e_prev = jnp.exp(cum - ld)
    e_end = jnp.exp(cum[c - 1:c, :] - cum)

    low2 = (lax.broadcasted_iota(jnp.int32, (c, 2 * c), 0)
            >= lax.broadcasted_iota(jnp.int32, (c, 2 * c), 1) % c)
    slow = _tril_mask(c, strict=True)
    eye = (lax.broadcasted_iota(jnp.int32, (c, c), 0)
           == lax.broadcasted_iota(jnp.int32, (c, c), 1)).astype(F32)
    hs = range(heads)
    sls = [slice(h * n, (h + 1) * n) for h in hs]
    np_ = 2 if A_PASSES == 3 else 1

    same_head = (lax.broadcasted_iota(jnp.int32, (width, width), 0) // n
                 == lax.broadcasted_iota(jnp.int32, (width, width), 1) // n).astype(BF16)
    yield
    ssq = None
    for part in reversed(_split(kk * kk, 3)):
        t = jnp.dot(part, same_head, preferred_element_type=F32)
        ssq = t if ssq is None else ssq + t
    kk = kk / jnp.maximum(jnp.sqrt(ssq), 1e-12)
    b = kk * a
    lhs_f = jnp.concatenate([-kk * e_prev, r * e_pos], axis=0)
    rhs_f = jnp.concatenate([b * e_neg, k2 * e_neg], axis=0)
    end_f = jnp.concatenate([b * e_end, k2 * e_end], axis=0)
    lhs = [_split(lhs_f[:, sl], np_) for sl in sls]
    rhs = [_split(rhs_f[:, sl], np_) for sl in sls]

    yield
    amat = [_mm_parts(lhs[h], rhs[h], NT) for h in hs]
    s0 = [s_ref[i, h] for h in hs]
    ls = [_mm_parts(lhs[h], _split(s0[h], np_), NT) for h in hs]

    yield
    a_ab = [jnp.where(slow, amat[h][0:c, 0:c], 0.0) for h in hs]
    rhs_u = [ls[h][0:c] + _mm(jnp.where(slow, amat[h][0:c, c:2 * c], 0.0), v[:, sls[h]],
                              NN, A_PASSES) for h in hs]

    yield
    tinv = [eye + a_ab[h] for h in hs]
    pw = [_mm(a_ab[h], a_ab[h], NN, A_INV_PASSES) for h in hs]
    for _ in range((c - 1).bit_length() - 2):
        yield
        res = [_mm(jnp.concatenate([pw[h], tinv[h]], axis=0), pw[h], NN, A_INV_PASSES)
               for h in hs]
        pw = [res[h][0:c] for h in hs]
        tinv = [tinv[h] + res[h][c:2 * c] for h in hs]
    yield
    tinv = [tinv[h] + _mm(tinv[h], pw[h], NN, A_INV_PASSES) for h in hs]
    yield
    u = [_mm(tinv[h], rhs_u[h], NN, A_INV_PASSES) for h in hs]
    yield
    uv = [_split(jnp.concatenate([u[h], v[:, sls[h]]], axis=0), np_) for h in hs]
    y = [ls[h][c:2 * c] + _mm_parts(_split(jnp.where(low2, amat[h][c:2 * c, :], 0.0), np_),
                                    uv[h], NN) for h in hs]
    for h in hs:
        sl = sls[h]
        s_ref[i, h] = s0[h] * e_pos[c - 1:c, sl] + _mm_parts(uv[h], _split(end_f[:, sl], np_), TN)

    yield
    for h in hs:
        sl = sls[h]
        mean = jnp.mean(y[h], axis=-1, keepdims=True)
        var = jnp.mean(jnp.square(y[h] - mean), axis=-1, keepdims=True)
        yn = (y[h] - mean) * lax.rsqrt(var + A_GN_EPS)
        yn = yn * lnw_ref[:, sl] + lnb_ref[:, sl]
        bonus = jnp.sum(r[:, sl] * k2[:, sl] * rk_ref[:, sl], axis=-1, keepdims=True) * v[:, sl]
        o_ref[i, :, sl] = ((yn + bonus) * g[:, sl]).astype(o_ref.dtype)


def _rwkv_mixer(p, mu, w0, w2, a0, a2, g2, k_k, k_a, r_k, ln_w, ln_b, out_dtype=BF16):
    bsz, lp, cols = p.shape
    nb = _seqs_per_step(bsz, A_SEQS_PER_STEP)
    width = w0.shape[-1]
    heads = width // A_HEAD_DIM
    kern = functools.partial(_rwkv_kernel, width=width, d_rank=w2.shape[0],
                             a_rank=a2.shape[0], g_rank=g2.shape[0])
    full = lambda arr: pl.BlockSpec(arr.shape, lambda b, c: (0,) * arr.ndim)
    small = (mu, w0, w2, a0, a2, g2, k_k, k_a, r_k, ln_w, ln_b)
    return pl.pallas_call(
        kern,
        grid=(bsz // nb, lp // CHUNK),
        in_specs=[pl.BlockSpec((nb, CHUNK, cols), lambda b, c: (b, c, 0))]
        + [full(t) for t in small],
        out_specs=pl.BlockSpec((nb, CHUNK, width), lambda b, c: (b, c, 0)),
        out_shape=jax.ShapeDtypeStruct((bsz, lp, width), out_dtype),
        scratch_shapes=[pltpu.VMEM((nb, heads, A_HEAD_DIM, A_HEAD_DIM), F32),
                        pltpu.VMEM((nb, 1, cols), F32)],
        compiler_params=_params(2),
    )(p, *small)


def _gla_kernel(p_ref, *rest, qk_width, v_width, rank):
    s_ref = rest[-1]

    @pl.when(pl.program_id(1) == 0)
    def _():
        s_ref[...] = jnp.zeros_like(s_ref)

    _round_robin(_gla_chunk(i, p_ref, *rest, qk_width=qk_width, v_width=v_width, rank=rank)
                 for i in range(p_ref.shape[0]))


def _gla_chunk(i, p_ref, a2_ref, ab_ref, nw_ref, o_ref, s_ref, *, qk_width, v_width, rank):
    c = CHUNK
    dv = B_DV
    heads = v_width // dv
    dk = qk_width // heads

    off_v = 2 * qk_width
    off_a = off_v + v_width
    off_g = off_a + _rup(rank, LANE)
    p = p_ref[i]
    q = p[:, 0:qk_width] * (dk ** -0.5)
    k = p[:, qk_width:off_v]
    v = p[:, off_v:off_a]
    xa = p[:, off_a:off_a + rank]
    gate = p[:, off_g:off_g + v_width]

    lg = _log_sigmoid(_mm(xa, a2_ref[...], NN, GATE_PASSES) + ab_ref[...]) / B_GATE_NORMALIZER
    yield
    bcum = _cumsum_rows(lg)
    low = _tril_mask(c)
    b_last = bcum[c - 1:c, :]
    q_in = q * jnp.exp(bcum)
    k_end = k * jnp.exp(b_last - bcum)
    e_last = jnp.exp(b_last)

    qf, kf = [], []
    for i0 in range(0, c, B_SUB):
        i1 = i0 + B_SUB
        ref_pt = bcum[i1 - 1:i1, :]
        qf.append(q[i0:i1] * jnp.exp(bcum[i0:i1] - ref_pt))
        kf.append(k[0:i1] * jnp.exp(ref_pt - bcum[0:i1]))

    hs = range(heads)
    sls = [slice(h * dk, (h + 1) * dk) for h in hs]
    vss = [slice(h * dv, (h + 1) * dv) for h in hs]
    yield
    st = [s_ref[i, h] for h in hs]
    o = [_mm(q_in[:, sls[h]], st[h], NT, B_PASSES) for h in hs]
    starts = list(range(0, c, B_SUB))
    sc = [[_mm(qf[bi][:, sls[h]], kf[bi][:, sls[h]], NT, B_PASSES)
           for bi in range(len(starts))] for h in hs]
    yield
    sc = [[jnp.where(low[i0:i0 + B_SUB, 0:i0 + B_SUB], sc[h][bi], 0.0)
           for bi, i0 in enumerate(starts)] for h in hs]
    for h in hs:
        blocks = [_mm(sc[h][bi], v[0:i0 + B_SUB, vss[h]], NN, B_PASSES)
                  for bi, i0 in enumerate(starts)]
        o[h] = o[h] + jnp.concatenate(blocks, axis=0)
    for h in hs:
        s_ref[i, h] = st[h] * e_last[:, sls[h]] + _mm(v[:, vss[h]], k_end[:, sls[h]], TN, B_PASSES)
    yield
    for h in hs:
        vs = vss[h]
        o_h = o[h] * lax.rsqrt(jnp.mean(o[h] * o[h], axis=-1, keepdims=True) + EPS) * nw_ref[...]
        g_h = gate[:, vs]
        o_ref[i, :, vs] = (o_h * (g_h * jax.nn.sigmoid(g_h))).astype(o_ref.dtype)


def _gla_mixer(p, a2, ab, norm_w, out_dtype=BF16):
    bsz, lp, cols = p.shape
    rank, qk_width = a2.shape
    dv = norm_w.shape[-1]
    heads = qk_width // (dv // 2)
    v_width = heads * dv
    nb = _seqs_per_step(bsz, B_SEQS_PER_STEP)
    kern = functools.partial(_gla_kernel, qk_width=qk_width, v_width=v_width, rank=rank)
    full = lambda arr: pl.BlockSpec(arr.shape, lambda b, c: (0,) * arr.ndim)
    small = (a2, ab, norm_w)
    return pl.pallas_call(
        kern,
        grid=(bsz // nb, lp // CHUNK),
        in_specs=[pl.BlockSpec((nb, CHUNK, cols), lambda b, c: (b, c, 0))]
        + [full(t) for t in small],
        out_specs=pl.BlockSpec((nb, CHUNK, v_width), lambda b, c: (b, c, 0)),
        out_shape=jax.ShapeDtypeStruct((bsz, lp, v_width), out_dtype),
        scratch_shapes=[pltpu.VMEM((nb, heads, dv, qk_width // heads), F32)],
        compiler_params=_params(2),
    )(p, *small)


def _mlstm_kernel(p_ref, *rest, qk_width, v_width, heads):
    @pl.when(pl.program_id(1) == 0)
    def _():
        for ref in rest[-4:]:
            ref[...] = jnp.zeros_like(ref)

    _round_robin(_mlstm_chunk(i, p_ref, *rest, qk_width=qk_width, v_width=v_width,
                              heads=heads) for i in range(p_ref.shape[0]))


def _mlstm_chunk(i, p_ref, cw_ref, cb_ref, gb_ref, nw_ref, o_ref, cm_ref, n_ref, m_ref,
                 hist_ref, *, qk_width, v_width, heads):
    c = CHUNK
    dqk = qk_width // heads
    dv = v_width // heads

    off_v = 2 * qk_width
    off_g = off_v + v_width
    off_o = off_g + LANE
    p = p_ref[i]
    qk_raw = p[:, 0:off_v]
    v = p[:, off_v:off_g]
    gates = p[:, off_g:off_o]
    og = p[:, off_o:off_o + v_width]

    ext = jnp.concatenate([hist_ref[i], qk_raw], axis=0)
    hist_ref[i] = qk_raw[c - 8:c, :]
    conv = cb_ref[...] + ext * cw_ref[C_CONV - 1:C_CONV, :]
    for d in range(1, C_CONV):
        conv = conv + pltpu.roll(ext, d, axis=0) * cw_ref[C_CONV - 1 - d:C_CONV - d, :]
    conv = conv[8:, :]
    qk = conv * jax.nn.sigmoid(conv)
    q = qk[:, 0:qk_width]
    k = qk[:, qk_width:off_v] * (dqk ** -0.5)

    capped = C_GATE_CAP * jnp.tanh((gates + gb_ref[...]) / C_GATE_CAP)
    lf = _log_sigmoid(capped)
    yield
    bcol = _cumsum_rows(lf)
    brow = bcol.T
    irow = capped.T
    low = _tril_mask(c)
    hs = range(heads)
    qss = [slice(h * dqk, (h + 1) * dqk) for h in hs]
    vss = [slice(h * dv, (h + 1) * dv) for h in hs]
    np_ = 2 if C_PASSES == 3 else 1

    q_p = [_split(q[:, qss[h]], np_) for h in hs]
    k_p = [_split(k[:, qss[h]], np_) for h in hs]
    yield
    cm = [cm_ref[i, h] for h in hs]
    qk_s = [_mm_parts(q_p[h], k_p[h], NT) for h in hs]
    q_cm = [_mm_parts(q_p[h], _split(cm[h], np_), NN) for h in hs]

    yield
    bc = [bcol[:, heads + h:heads + h + 1] for h in hs]
    ic = [capped[:, h:h + 1] for h in hs]
    m_prev = [m_ref[i, h:h + 1, 0:1] for h in hs]
    n_row = [n_ref[i, h:h + 1, :] for h in hs]
    log_w = [jnp.where(low, bc[h] - brow[heads + h:heads + h + 1, :] + irow[h:h + 1, :],
                       -jnp.inf) for h in hs]
    log_prev = [bc[h] + m_prev[h] for h in hs]
    m_t = [jnp.maximum(log_prev[h], jnp.max(log_w[h], axis=-1, keepdims=True)) for h in hs]
    w_prev = [jnp.exp(log_prev[h] - m_t[h]) for h in hs]
    s = [qk_s[h] * jnp.exp(log_w[h] - m_t[h]) for h in hs]
    yield
    num = [w_prev[h] * q_cm[h] + _mm(s[h], v[:, vss[h]], NN, C_PASSES) for h in hs]
    den = [w_prev[h] * jnp.sum(q[:, qss[h]] * n_row[h], axis=-1, keepdims=True)
           + jnp.sum(s[h], axis=-1, keepdims=True) for h in hs]
    hh = [num[h] / jnp.maximum(jnp.abs(den[h]), jnp.exp(-m_t[h])) for h in hs]

    yield
    m_new = [m_t[h][c - 1:c, :] for h in hs]
    b_last = [bc[h][c - 1:c, :] for h in hs]
    kw = [k[:, qss[h]] * jnp.exp(b_last[h] - bc[h] + ic[h] - m_new[h]) for h in hs]
    f_end = [jnp.exp(b_last[h] + m_prev[h] - m_new[h]) for h in hs]
    for h in hs:
        cm_ref[i, h] = f_end[h] * cm[h] + _mm(kw[h], v[:, vss[h]], TN, C_PASSES)
        n_ref[i, h:h + 1, :] = f_end[h] * n_row[h] + jnp.sum(kw[h], axis=0, keepdims=True)
        m_ref[i, h:h + 1, :] = jnp.broadcast_to(m_new[h], (1, m_ref.shape[-1]))

    yield
    for h in hs:
        vs = vss[h]
        hn = hh[h] * lax.rsqrt(jnp.mean(hh[h] * hh[h], axis=-1, keepdims=True) + EPS)
        o_ref[i, :, vs] = (hn * nw_ref[:, vs] * jax.nn.sigmoid(og[:, vs])).astype(o_ref.dtype)


def _mlstm_mixer(p, conv_w, conv_b, gate_b, norm_w, heads, out_dtype=BF16):
    bsz, lp, cols = p.shape
    qk2 = conv_w.shape[-1]
    qk_width = qk2 // 2
    v_width = norm_w.shape[-1]
    nb = _seqs_per_step(bsz, C_SEQS_PER_STEP)
    kern = functools.partial(_mlstm_kernel, qk_width=qk_width, v_width=v_width, heads=heads)
    full = lambda arr: pl.BlockSpec(arr.shape, lambda b, c: (0,) * arr.ndim)
    small = (conv_w, conv_b, gate_b, norm_w)
    return pl.pallas_call(
        kern,
        grid=(bsz // nb, lp // CHUNK),
        in_specs=[pl.BlockSpec((nb, CHUNK, cols), lambda b, c: (b, c, 0))]
        + [full(t) for t in small],
        out_specs=pl.BlockSpec((nb, CHUNK, v_width), lambda b, c: (b, c, 0)),
        out_shape=jax.ShapeDtypeStruct((bsz, lp, v_width), out_dtype),
        scratch_shapes=[pltpu.VMEM((nb, heads, qk_width // heads, v_width // heads), F32),
                        pltpu.VMEM((nb, 8, qk_width // heads), F32),
                        pltpu.VMEM((nb, 8, LANE), F32),
                        pltpu.VMEM((nb, 8, qk2), F32)],
        compiler_params=_params(2),
    )(p, *small)


def _out_proj_kernel(h_ref, ya_ref, yb_ref, yc_ref, wa_ref, wb_ref, wc_ref, o_ref):
    acc = jnp.dot(ya_ref[...], wa_ref[...], preferred_element_type=F32)
    acc += jnp.dot(yb_ref[...], wb_ref[...], preferred_element_type=F32)
    acc += jnp.dot(yc_ref[...], wc_ref[...], preferred_element_type=F32)
    o_ref[...] = h_ref[...] + acc


def _out_proj(h, ya, yb, yc, wa, wb, wc, tm):
    m, d = h.shape
    row = lambda arr: pl.BlockSpec((tm, arr.shape[1]), lambda i: (i, 0))
    full = lambda arr: pl.BlockSpec(arr.shape, lambda i: (0, 0))
    return pl.pallas_call(
        _out_proj_kernel,
        grid=(m // tm,),
        in_specs=[row(h), row(ya), row(yb), row(yc), full(wa), full(wb), full(wc)],
        out_specs=pl.BlockSpec((tm, d), lambda i: (i, 0)),
        out_shape=jax.ShapeDtypeStruct((m, d), F32),
        compiler_params=_params(1),
    )(h, ya, yb, yc, wa, wb, wc)


def _ffn_kernel(h_ref, g_ref, w1_ref, w3_ref, w2_ref, gf_ref, o_ref, u_ref, *, final_norm):
    f = pl.program_id(1)

    @pl.when(f == 0)
    def _():
        x = h_ref[...]
        ms = jnp.mean(x * x, axis=-1, keepdims=True)
        u_ref[...] = (x * lax.rsqrt(ms + EPS) * g_ref[...]).astype(BF16)
        o_ref[...] = x

    u = u_ref[...]
    a = jnp.dot(u, w1_ref[...], preferred_element_type=F32)
    b = jnp.dot(u, w3_ref[...], preferred_element_type=F32)
    act = (a * jax.nn.sigmoid(a) * b).astype(BF16)
    o_ref[...] += jnp.dot(act, w2_ref[...], preferred_element_type=F32)

    if final_norm:
        @pl.when(f == pl.num_programs(1) - 1)
        def _():
            y = o_ref[...]
            ms = jnp.mean(y * y, axis=-1, keepdims=True)
            o_ref[...] = y * lax.rsqrt(ms + EPS) * gf_ref[...]


def _ffn(h, g, w1, w3, w2, g_final, tm, tf, final_norm):
    m, d = h.shape
    dff = w1.shape[1]
    return pl.pallas_call(
        functools.partial(_ffn_kernel, final_norm=final_norm),
        grid=(m // tm, dff // tf),
        in_specs=[
            pl.BlockSpec((tm, d), lambda i, f: (i, 0)),
            pl.BlockSpec((1, d), lambda i, f: (0, 0)),
            pl.BlockSpec((d, tf), lambda i, f: (0, f)),
            pl.BlockSpec((d, tf), lambda i, f: (0, f)),
            pl.BlockSpec((tf, d), lambda i, f: (f, 0)),
            pl.BlockSpec((1, d), lambda i, f: (0, 0)),
        ],
        out_specs=pl.BlockSpec((tm, d), lambda i, f: (i, 0)),
        out_shape=jax.ShapeDtypeStruct((m, d), F32),
        scratch_shapes=[pltpu.VMEM((tm, d), BF16)],
        compiler_params=_params(2),
    )(h, g, w1, w3, w2, g_final)


def _pad_groups(w, sizes, axis=-1):
    pieces = []
    off = 0
    for s in sizes:
        piece = lax.slice_in_dim(w, off, off + s, axis=axis)
        pad = _rup(s, LANE) - s
        if pad:
            cfg = [(0, 0)] * w.ndim
            cfg[axis] = (0, pad)
            piece = jnp.pad(piece, cfg)
        pieces.append(piece)
        off += s
    return jnp.concatenate(pieces, axis=axis)


def _tile_rows(m):
    for t in (640, 512, 320, 256, 128, 64, 32, 16, 8):
        if m % t == 0:
            return t
    return m


def _proj_tiles(m, d, n, in_bytes):
    budget = VMEM_LIMIT * 3 // 4
    col_tiles = [t for t in range(n, 0, -LANE) if n % t == 0]
    row_tiles = [t for t in (640, 320, 256, 128, 64, 32, 16, 8) if m % t == 0] or [m]
    for tn in col_tiles:
        for tm in row_tiles:
            need = 2 * tm * d * in_bytes + 2 * d * tn * 2 + 2 * tm * tn * 4
            if need <= budget:
                return tm, tn
    return row_tiles[-1], col_tiles[-1]


def _tile_cols(n, cap):
    best = LANE
    for t in range(LANE, cap + 1, LANE):
        if n % t == 0:
            best = t
    return best


def kernel(x, meta_tokens, norm_mix, w_in, rw_mu, rw_w0, rw_w2, rw_a0, rw_a2, rw_g2, rw_kk, rw_ka, rw_rk, rw_ln_w, rw_ln_b, gla_a2, gla_ab, gla_norm, ml_conv_w, ml_conv_b, ml_ib, ml_fb, ml_norm, w_out, norm_ffn, ffn_w1, ffn_w3, ffn_w2, norm_final):
    bsz, seq, d = x.shape
    depth = w_in.shape[0]
    n_meta = meta_tokens.shape[0]
    ltot = n_meta + seq
    lp = _rup(ltot, CHUNK)

    a_width = rw_w0.shape[-1]
    d_rank, a_rank, g_rank = rw_w2.shape[1], rw_a2.shape[1], rw_g2.shape[1]
    a_sizes = [a_width, a_width, a_width, d_rank, a_rank, g_rank]
    b_rank, b_qk = gla_a2.shape[1], gla_a2.shape[2]
    b_heads = b_qk // (gla_norm.shape[-1] // 2)
    b_width = b_heads * gla_norm.shape[-1]
    b_sizes = [b_qk, b_qk, b_width, b_rank, b_width]
    c_heads = ml_ib.shape[-1]
    c_qk = ml_conv_w.shape[-1] // 2
    c_width = ml_norm.shape[-1]
    c_sizes = [c_qk, c_qk, c_width, 2 * c_heads, c_width]
    a_cols, b_cols = sum(a_sizes), sum(b_sizes)

    meta = jnp.broadcast_to(meta_tokens[None].astype(x.dtype), (bsz, n_meta, d))
    h = jnp.concatenate([meta, x, jnp.zeros((bsz, lp - ltot, d), x.dtype)], axis=1)
    h = h.reshape(bsz * lp, d)
    m = bsz * lp
    tm = _tile_rows(m)
    row2 = lambda t: t.reshape(1, -1)

    w_in_bf, w_out_bf = w_in.astype(BF16), w_out.astype(BF16)
    w1_bf, w3_bf, w2_bf = ffn_w1.astype(BF16), ffn_w3.astype(BF16), ffn_w2.astype(BF16)
    for l in range(depth):
        w_l = w_in_bf[l]
        wa = _pad_groups(w_l[:, :a_cols], a_sizes)
        wb = _pad_groups(w_l[:, a_cols:a_cols + b_cols], b_sizes)
        wc = _pad_groups(w_l[:, a_cols + b_cols:], c_sizes)
        p_a, u = _norm_matmul(h, row2(norm_mix[l]), wa, *_proj_tiles(m, d, wa.shape[1], 6))
        p_b = _matmul(u, wb, *_proj_tiles(m, d, wb.shape[1], 2))
        p_c = _matmul(u, wc, *_proj_tiles(m, d, wc.shape[1], 2))

        y_a = _rwkv_mixer(
            p_a.reshape(bsz, lp, -1), _pad_groups(row2(rw_mu[l]), a_sizes),
            row2(rw_w0[l]), rw_w2[l], row2(rw_a0[l]), rw_a2[l], rw_g2[l], row2(rw_kk[l]),
            row2(rw_ka[l]), row2(rw_rk[l]), row2(rw_ln_w[l]), row2(rw_ln_b[l]))
        y_b = _gla_mixer(p_b.reshape(bsz, lp, -1), gla_a2[l], row2(gla_ab[l]),
                         row2(gla_norm[l]))
        gate_b = jnp.pad(jnp.concatenate([ml_ib[l], ml_fb[l]]), (0, LANE - 2 * c_heads))
        y_c = _mlstm_mixer(p_c.reshape(bsz, lp, -1), ml_conv_w[l], row2(ml_conv_b[l]),
                           row2(gate_b), row2(ml_norm[l]), c_heads)

        wo = w_out_bf[l]
        h = _out_proj(h, y_a.reshape(m, -1), y_b.reshape(m, -1), y_c.reshape(m, -1),
                      wo[:a_width], wo[a_width:a_width + b_width], wo[a_width + b_width:], tm)
        last = l == depth - 1
        h = _ffn(h, row2(norm_ffn[l]), w1_bf[l], w3_bf[l], w2_bf[l], row2(norm_final), tm,
                 _tile_cols(ffn_w1.shape[-1], 512), last)

    return h.reshape(bsz, lp, d)[:, n_meta:ltot]
```

```python
import functools

import jax
import jax.numpy as jnp
from jax import lax
from jax.experimental import pallas as pl
from jax.experimental.pallas import tpu as pltpu

F32 = jnp.float32
BF16 = jnp.bfloat16

EPS = 1e-6
N_META = 16
CHUNK = 64
A_SEQS_PER_STEP = 4
B_SEQS_PER_STEP = 4
C_SEQS_PER_STEP = 1

A_HEAD_DIM = 64
A_GN_EPS = 64e-5
B_DV = 128
B_GATE_NORMALIZER = 16.0
B_SUB = 16
C_DV = 256
C_CONV = 4
C_GATE_CAP = 15.0

LANE = 128
VMEM_LIMIT = 56 * 1024 * 1024

NN = (((1,), (0,)), ((), ()))
NT = (((1,), (1,)), ((), ()))
TN = (((0,), (0,)), ((), ()))

GATE_PASSES = 3
A_PASSES = 1
A_INV_PASSES = 1
B_PASSES = 1
C_PASSES = 1


def _rup(n, m):
    return -(-n // m) * m


def _split(x, parts):
    out = []
    rem = x
    for i in range(parts):
        hi = rem.astype(BF16)
        out.append(hi)
        if i + 1 < parts:
            rem = rem - hi.astype(F32)
    return out


def _mm_parts(a_parts, b_parts, dims):
    dg = lambda a, b: lax.dot_general(a, b, dims, preferred_element_type=F32)
    acc = dg(a_parts[0], b_parts[0])
    if len(a_parts) > 1:
        acc = acc + dg(a_parts[0], b_parts[1]) + dg(a_parts[1], b_parts[0])
    return acc


def _mm(a, b, dims=NN, passes=1):
    n = 2 if passes == 3 else 1
    return _mm_parts(_split(a, n), _split(b, n), dims)


def _cumsum_rows(x):
    tri = _tril_mask(x.shape[0]).astype(BF16)
    hi, mid, lo = (jnp.dot(tri, part, preferred_element_type=F32) for part in _split(x, 3))
    return hi + (mid + lo)


def _softplus(z):
    return jnp.maximum(z, 0.0) + jnp.log1p(jnp.exp(-jnp.abs(z)))


def _log_sigmoid(z):
    return -_softplus(-z)


def _tril_mask(n, strict=False):
    r = lax.broadcasted_iota(jnp.int32, (n, n), 0)
    c = lax.broadcasted_iota(jnp.int32, (n, n), 1)
    return (r > c) if strict else (r >= c)


def _params(n_axes):
    return pltpu.CompilerParams(
        dimension_semantics=("arbitrary",) * n_axes, vmem_limit_bytes=VMEM_LIMIT)


def _norm_matmul_kernel(h_ref, g_ref, w_ref, o_ref, u_ref):
    @pl.when(pl.program_id(1) == 0)
    def _():
        x = h_ref[...]
        ms = jnp.mean(x * x, axis=-1, keepdims=True)
        u_ref[...] = (x * lax.rsqrt(ms + EPS) * g_ref[...]).astype(BF16)

    o_ref[...] = jnp.dot(u_ref[...], w_ref[...], preferred_element_type=F32)


def _norm_matmul(h, g, w, tm, tn):
    m, d = h.shape
    n = w.shape[1]
    return pl.pallas_call(
        _norm_matmul_kernel,
        grid=(m // tm, n // tn),
        in_specs=[
            pl.BlockSpec((tm, d), lambda i, j: (i, 0)),
            pl.BlockSpec((1, d), lambda i, j: (0, 0)),
            pl.BlockSpec((d, tn), lambda i, j: (0, j)),
        ],
        out_specs=[pl.BlockSpec((tm, tn), lambda i, j: (i, j)),
                   pl.BlockSpec((tm, d), lambda i, j: (i, 0))],
        out_shape=[jax.ShapeDtypeStruct((m, n), F32), jax.ShapeDtypeStruct((m, d), BF16)],
        compiler_params=_params(2),
    )(h, g, w)


def _matmul_kernel(u_ref, w_ref, o_ref):
    o_ref[...] = jnp.dot(u_ref[...], w_ref[...], preferred_element_type=F32)


def _matmul(u, w, tm, tn):
    m, d = u.shape
    n = w.shape[1]
    return pl.pallas_call(
        _matmul_kernel,
        grid=(m // tm, n // tn),
        in_specs=[pl.BlockSpec((tm, d), lambda i, j: (i, 0)),
                  pl.BlockSpec((d, tn), lambda i, j: (0, j))],
        out_specs=pl.BlockSpec((tm, tn), lambda i, j: (i, j)),
        out_shape=jax.ShapeDtypeStruct((m, n), F32),
        compiler_params=_params(2),
    )(u, w)


def _seqs_per_step(bsz, want):
    return want if bsz % want == 0 else 1


def _round_robin(gens):
    gens = list(gens)
    while gens:
        alive = []
        for gen in gens:
            try:
                next(gen)
                alive.append(gen)
            except StopIteration:
                pass
        gens = alive


def _rwkv_kernel(p_ref, *rest, width, d_rank, a_rank, g_rank):
    s_ref, prev_ref = rest[-2:]

    @pl.when(pl.program_id(1) == 0)
    def _():
        s_ref[...] = jnp.zeros_like(s_ref)
        prev_ref[...] = jnp.zeros_like(prev_ref)

    _round_robin(_rwkv_chunk(i, p_ref, *rest, width=width, d_rank=d_rank, a_rank=a_rank,
                             g_rank=g_rank) for i in range(p_ref.shape[0]))


def _rwkv_chunk(i, p_ref, mu_ref, w0_ref, w2_ref, a0_ref, a2_ref, g2_ref, kk_ref, ka_ref,
                rk_ref, lnw_ref, lnb_ref, o_ref, s_ref, prev_ref, *, width, d_rank,
                a_rank, g_rank):
    c = CHUNK
    n = A_HEAD_DIM
    heads = width // n

    p = p_ref[i]
    row = lax.broadcasted_iota(jnp.int32, p.shape, 0)
    shifted = jnp.where(row == 0, prev_ref[i], pltpu.roll(p, 1, axis=0))
    prev_ref[i] = p[c - 1:c, :]
    x = p + mu_ref[...] * (shifted - p)

    off_w = 3 * width
    off_a = off_w + _rup(d_rank, LANE)
    off_g = off_a + _rup(a_rank, LANE)
    r = x[:, 0:width]
    k = x[:, width:2 * width]
    v = x[:, 2 * width:3 * width]
    xw = x[:, off_w:off_w + d_rank]
    xa = x[:, off_a:off_a + a_rank]
    xg = x[:, off_g:off_g + g_rank]

    w_log = -_softplus(-(w0_ref[...] + _mm(jnp.tanh(xw), w2_ref[...], NN, GATE_PASSES))) - 0.5
    ld = -jnp.exp(w_log)
    a = jax.nn.sigmoid(a0_ref[...] + _mm(xa, a2_ref[...], NN, GATE_PASSES))
    g = _mm(jax.nn.sigmoid(xg), g2_ref[...], NN, GATE_PASSES)
    kk = k * kk_ref[...]
    k2 = k * (1.0 + (a - 1.0) * ka_ref[...])

    yield
    cum = _cumsum_rows(ld)
    e_pos = jnp.exp(cum)
    e_neg = jnp.exp(-cum)
    e_prev = jnp.exp(cum - ld)
    e_end = jnp.exp(cum[c - 1:c, :] - cum)

    assert c == n and 2 * n == LANE
    pairs = heads // 2
    ps = [slice(j * LANE, (j + 1) * LANE) for j in range(pairs)]
    js = range(pairs)
    first = lax.broadcasted_iota(jnp.int32, (1, LANE), 1) < n
    rows = lax.broadcasted_iota(jnp.int32, (c, 4 * c), 0)
    cols = lax.broadcasted_iota(jnp.int32, (c, 4 * c), 1) % c
    low4 = rows >= cols
    slow2 = (rows > cols)[:, 0:LANE]
    eye2 = (rows == cols)[:, 0:LANE].astype(F32)

    def bdiag(x):
        return jnp.concatenate([jnp.where(first, x, 0.0), jnp.where(first, 0.0, x)], axis=0)

    same_head = (lax.broadcasted_iota(jnp.int32, (LANE, LANE), 0) // n
                 == lax.broadcasted_iota(jnp.int32, (LANE, LANE), 1) // n).astype(BF16)

    def head_sum(x, parts):
        acc = None
        for part in reversed(_split(x, parts)):
            t = jnp.concatenate([jnp.dot(part[:, sl], same_head, preferred_element_type=F32)
                                 for sl in ps], axis=1)
            acc = t if acc is None else acc + t
        return acc

    yield
    kk = kk / jnp.maximum(jnp.sqrt(head_sum(kk * kk, 2)), 1e-12)
    b = kk * a
    lhs_f = jnp.concatenate([-kk * e_prev, r * e_pos], axis=0)
    rhs_f = jnp.concatenate([b * e_neg, k2 * e_neg], axis=0)
    end_f = jnp.concatenate([b * e_end, k2 * e_end], axis=0)

    yield
    s0 = [s_ref[i, j] for j in js]
    big = [_mm(lhs_f[:, ps[j]],
               jnp.concatenate([bdiag(rhs_f[0:c, ps[j]]), bdiag(rhs_f[c:2 * c, ps[j]]),
                                bdiag(s0[j])], axis=0), NT, A_PASSES) for j in js]

    yield
    a_ab = [jnp.where(slow2, big[j][0:c, 0:LANE], 0.0) for j in js]
    v_bd = [bdiag(v[:, ps[j]]) for j in js]
    rhs_u = [big[j][0:c, 2 * LANE:3 * LANE]
             + _mm(jnp.where(slow2, big[j][0:c, LANE:2 * LANE], 0.0), v_bd[j], NN, A_PASSES)
             for j in js]

    yield
    tinv = [eye2 + a_ab[j] for j in js]
    pw = [_mm(a_ab[j], bdiag(a_ab[j]), NN, A_INV_PASSES) for j in js]
    for _ in range((c - 1).bit_length() - 2):
        yield
        res = [_mm(jnp.concatenate([pw[j], tinv[j]], axis=0), bdiag(pw[j]), NN, A_INV_PASSES)
               for j in js]
        pw = [res[j][0:c] for j in js]
        tinv = [tinv[j] + res[j][c:2 * c] for j in js]
    yield
    tinv = [tinv[j] + _mm(tinv[j], bdiag(pw[j]), NN, A_INV_PASSES) for j in js]
    yield
    u = [_mm(tinv[j], bdiag(rhs_u[j]), NN, A_INV_PASSES) for j in js]
    yield
    y = [big[j][c:2 * c, 2 * LANE:3 * LANE]
         + _mm(jnp.where(low4, big[j][c:2 * c, 0:2 * LANE], 0.0),
               jnp.concatenate([bdiag(u[j]), v_bd[j]], axis=0), NN, A_PASSES) for j in js]
    for j in js:
        uv = jnp.concatenate([u[j], v[:, ps[j]]], axis=0)
        full = _mm(uv, end_f[:, ps[j]], TN, A_PASSES)
        s_ref[i, j] = (s0[j] * e_pos[c - 1:c, ps[j]]
                       + jnp.where(first, full[0:n], full[n:2 * n]))

    yield
    y = jnp.concatenate(y, axis=1)
    sums = head_sum(jnp.concatenate([y, r * k2 * rk_ref[...]], axis=0), 2)
    yc = y - sums[0:c] * (1.0 / n)
    var = head_sum(yc * yc, 2) * (1.0 / n)
    yn = yc * lax.rsqrt(var + A_GN_EPS) * lnw_ref[...] + lnb_ref[...]
    bonus = sums[c:2 * c] * v
    o_ref[i] = ((yn + bonus) * g).astype(o_ref.dtype)


def _rwkv_mixer(p, mu, w0, w2, a0, a2, g2, k_k, k_a, r_k, ln_w, ln_b, out_dtype=BF16):
    bsz, lp, cols = p.shape
    nb = _seqs_per_step(bsz, A_SEQS_PER_STEP)
    width = w0.shape[-1]
    heads = width // A_HEAD_DIM
    kern = functools.partial(_rwkv_kernel, width=width, d_rank=w2.shape[0],
                             a_rank=a2.shape[0], g_rank=g2.shape[0])
    full = lambda arr: pl.BlockSpec(arr.shape, lambda b, c: (0,) * arr.ndim)
    small = (mu, w0, w2, a0, a2, g2, k_k, k_a, r_k, ln_w, ln_b)
    return pl.pallas_call(
        kern,
        grid=(bsz // nb, lp // CHUNK),
        in_specs=[pl.BlockSpec((nb, CHUNK, cols), lambda b, c: (b, c, 0))]
        + [full(t) for t in small],
        out_specs=pl.BlockSpec((nb, CHUNK, width), lambda b, c: (b, c, 0)),
        out_shape=jax.ShapeDtypeStruct((bsz, lp, width), out_dtype),
        scratch_shapes=[pltpu.VMEM((nb, heads // 2, A_HEAD_DIM, 2 * A_HEAD_DIM), F32),
                        pltpu.VMEM((nb, 1, cols), F32)],
        compiler_params=_params(2),
    )(p, *small)


def _gla_kernel(p_ref, *rest, qk_width, v_width, rank):
    s_ref = rest[-1]

    @pl.when(pl.program_id(1) == 0)
    def _():
        s_ref[...] = jnp.zeros_like(s_ref)

    _round_robin(_gla_chunk(i, p_ref, *rest, qk_width=qk_width, v_width=v_width, rank=rank)
                 for i in range(p_ref.shape[0]))


def _gla_chunk(i, p_ref, a2_ref, ab_ref, nw_ref, o_ref, s_ref, *, qk_width, v_width, rank):
    c = CHUNK
    dv = B_DV
    heads = v_width // dv
    dk = qk_width // heads

    off_v = 2 * qk_width
    off_a = off_v + v_width
    off_g = off_a + _rup(rank, LANE)
    p = p_ref[i]
    q = p[:, 0:qk_width] * (dk ** -0.5)
    k = p[:, qk_width:off_v]
    v = p[:, off_v:off_a]
    xa = p[:, off_a:off_a + rank]
    gate = p[:, off_g:off_g + v_width]

    lg = _log_sigmoid(_mm(xa, a2_ref[...], NN, GATE_PASSES) + ab_ref[...]) / B_GATE_NORMALIZER
    yield
    bcum = _cumsum_rows(lg)
    low = _tril_mask(c)
    b_last = bcum[c - 1:c, :]
    q_in = q * jnp.exp(bcum)
    k_end = k * jnp.exp(b_last - bcum)
    e_last = jnp.exp(b_last)

    qf, kf = [], []
    for i0 in range(0, c, B_SUB):
        i1 = i0 + B_SUB
        ref_pt = bcum[i1 - 1:i1, :]
        qf.append(q[i0:i1] * jnp.exp(bcum[i0:i1] - ref_pt))
        kf.append(k[0:i1] * jnp.exp(ref_pt - bcum[0:i1]))

    hs = range(heads)
    sls = [slice(h * dk, (h + 1) * dk) for h in hs]
    vss = [slice(h * dv, (h + 1) * dv) for h in hs]
    yield
    st = [s_ref[i, h] for h in hs]
    o = [_mm(q_in[:, sls[h]], st[h], NT, B_PASSES) for h in hs]
    starts = list(range(0, c, B_SUB))
    sc = [[_mm(qf[bi][:, sls[h]], kf[bi][:, sls[h]], NT, B_PASSES)
           for bi in range(len(starts))] for h in hs]
    yield
    sc = [[jnp.where(low[i0:i0 + B_SUB, 0:i0 + B_SUB], sc[h][bi], 0.0)
           for bi, i0 in enumerate(starts)] for h in hs]
    for h in hs:
        blocks = [_mm(sc[h][bi], v[0:i0 + B_SUB, vss[h]], NN, B_PASSES)
                  for bi, i0 in enumerate(starts)]
        o[h] = o[h] + jnp.concatenate(blocks, axis=0)
    for h in hs:
        s_ref[i, h] = st[h] * e_last[:, sls[h]] + _mm(v[:, vss[h]], k_end[:, sls[h]], TN, B_PASSES)
    yield
    for h in hs:
        vs = vss[h]
        o_h = o[h] * lax.rsqrt(jnp.mean(o[h] * o[h], axis=-1, keepdims=True) + EPS) * nw_ref[...]
        g_h = gate[:, vs]
        o_ref[i, :, vs] = (o_h * (g_h * jax.nn.sigmoid(g_h))).astype(o_ref.dtype)


def _gla_mixer(p, a2, ab, norm_w, out_dtype=BF16):
    bsz, lp, cols = p.shape
    rank, qk_width = a2.shape
    dv = norm_w.shape[-1]
    heads = qk_width // (dv // 2)
    v_width = heads * dv
    nb = _seqs_per_step(bsz, B_SEQS_PER_STEP)
    kern = functools.partial(_gla_kernel, qk_width=qk_width, v_width=v_width, rank=rank)
    full = lambda arr: pl.BlockSpec(arr.shape, lambda b, c: (0,) * arr.ndim)
    small = (a2, ab, norm_w)
    return pl.pallas_call(
        kern,
        grid=(bsz // nb, lp // CHUNK),
        in_specs=[pl.BlockSpec((nb, CHUNK, cols), lambda b, c: (b, c, 0))]
        + [full(t) for t in small],
        out_specs=pl.BlockSpec((nb, CHUNK, v_width), lambda b, c: (b, c, 0)),
        out_shape=jax.ShapeDtypeStruct((bsz, lp, v_width), out_dtype),
        scratch_shapes=[pltpu.VMEM((nb, heads, dv, qk_width // heads), F32)],
        compiler_params=_params(2),
    )(p, *small)


def _mlstm_kernel(p_ref, *rest, qk_width, v_width, heads):
    @pl.when(pl.program_id(1) == 0)
    def _():
        for ref in rest[-4:]:
            ref[...] = jnp.zeros_like(ref)

    _round_robin(_mlstm_chunk(i, p_ref, *rest, qk_width=qk_width, v_width=v_width,
                              heads=heads) for i in range(p_ref.shape[0]))


def _mlstm_chunk(i, p_ref, cw_ref, cb_ref, gb_ref, nw_ref, o_ref, cm_ref, n_ref, m_ref,
                 hist_ref, *, qk_width, v_width, heads):
    c = CHUNK
    dqk = qk_width // heads
    dv = v_width // heads

    off_v = 2 * qk_width
    off_g = off_v + v_width
    off_o = off_g + LANE
    p = p_ref[i]
    qk_raw = p[:, 0:off_v]
    v = p[:, off_v:off_g]
    gates = p[:, off_g:off_o]
    og = p[:, off_o:off_o + v_width]

    ext = jnp.concatenate([hist_ref[i], qk_raw], axis=0)
    hist_ref[i] = qk_raw[c - 8:c, :]
    conv = cb_ref[...] + ext * cw_ref[C_CONV - 1:C_CONV, :]
    for d in range(1, C_CONV):
        conv = conv + pltpu.roll(ext, d, axis=0) * cw_ref[C_CONV - 1 - d:C_CONV - d, :]
    conv = conv[8:, :]
    qk = conv * jax.nn.sigmoid(conv)
    q = qk[:, 0:qk_width]
    k = qk[:, qk_width:off_v] * (dqk ** -0.5)

    capped = C_GATE_CAP * jnp.tanh((gates + gb_ref[...]) / C_GATE_CAP)
    lf = _log_sigmoid(capped)
    yield
    bcol = _cumsum_rows(lf)
    brow = bcol.T
    irow = capped.T
    low = _tril_mask(c)
    hs = range(heads)
    qss = [slice(h * dqk, (h + 1) * dqk) for h in hs]
    vss = [slice(h * dv, (h + 1) * dv) for h in hs]
    np_ = 2 if C_PASSES == 3 else 1

    q_p = [_split(q[:, qss[h]], np_) for h in hs]
    k_p = [_split(k[:, qss[h]], np_) for h in hs]
    yield
    cm = [cm_ref[i, h] for h in hs]
    qk_s = [_mm_parts(q_p[h], k_p[h], NT) for h in hs]
    q_cm = [_mm_parts(q_p[h], _split(cm[h], np_), NN) for h in hs]

    yield
    bc = [bcol[:, heads + h:heads + h + 1] for h in hs]
    ic = [capped[:, h:h + 1] for h in hs]
    m_prev = [m_ref[i, h:h + 1, 0:1] for h in hs]
    n_row = [n_ref[i, h:h + 1, :] for h in hs]
    log_w = [jnp.where(low, bc[h] - brow[heads + h:heads + h + 1, :] + irow[h:h + 1, :],
                       -jnp.inf) for h in hs]
    log_prev = [bc[h] + m_prev[h] for h in hs]
    m_t = [jnp.maximum(log_prev[h], jnp.max(log_w[h], axis=-1, keepdims=True)) for h in hs]
    w_prev = [jnp.exp(log_prev[h] - m_t[h]) for h in hs]
    s = [qk_s[h] * jnp.exp(log_w[h] - m_t[h]) for h in hs]
    yield
    num = [w_prev[h] * q_cm[h] + _mm(s[h], v[:, vss[h]], NN, C_PASSES) for h in hs]
    den = [w_prev[h] * jnp.sum(q[:, qss[h]] * n_row[h], axis=-1, keepdims=True)
           + jnp.sum(s[h], axis=-1, keepdims=True) for h in hs]
    hh = [num[h] / jnp.maximum(jnp.abs(den[h]), jnp.exp(-m_t[h])) for h in hs]

    yield
    m_new = [m_t[h][c - 1:c, :] for h in hs]
    b_last = [bc[h][c - 1:c, :] for h in hs]
    kw = [k[:, qss[h]] * jnp.exp(b_last[h] - bc[h] + ic[h] - m_new[h]) for h in hs]
    f_end = [jnp.exp(b_last[h] + m_prev[h] - m_new[h]) for h in hs]
    for h in hs:
        cm_ref[i, h] = f_end[h] * cm[h] + _mm(kw[h], v[:, vss[h]], TN, C_PASSES)
        n_ref[i, h:h + 1, :] = f_end[h] * n_row[h] + jnp.sum(kw[h], axis=0, keepdims=True)
        m_ref[i, h:h + 1, :] = jnp.broadcast_to(m_new[h], (1, m_ref.shape[-1]))

    yield
    for h in hs:
        vs = vss[h]
        hn = hh[h] * lax.rsqrt(jnp.mean(hh[h] * hh[h], axis=-1, keepdims=True) + EPS)
        o_ref[i, :, vs] = (hn * nw_ref[:, vs] * jax.nn.sigmoid(og[:, vs])).astype(o_ref.dtype)


def _mlstm_mixer(p, conv_w, conv_b, gate_b, norm_w, heads, out_dtype=BF16):
    bsz, lp, cols = p.shape
    qk2 = conv_w.shape[-1]
    qk_width = qk2 // 2
    v_width = norm_w.shape[-1]
    nb = _seqs_per_step(bsz, C_SEQS_PER_STEP)
    kern = functools.partial(_mlstm_kernel, qk_width=qk_width, v_width=v_width, heads=heads)
    full = lambda arr: pl.BlockSpec(arr.shape, lambda b, c: (0,) * arr.ndim)
    small = (conv_w, conv_b, gate_b, norm_w)
    return pl.pallas_call(
        kern,
        grid=(bsz // nb, lp // CHUNK),
        in_specs=[pl.BlockSpec((nb, CHUNK, cols), lambda b, c: (b, c, 0))]
        + [full(t) for t in small],
        out_specs=pl.BlockSpec((nb, CHUNK, v_width), lambda b, c: (b, c, 0)),
        out_shape=jax.ShapeDtypeStruct((bsz, lp, v_width), out_dtype),
        scratch_shapes=[pltpu.VMEM((nb, heads, qk_width // heads, v_width // heads), F32),
                        pltpu.VMEM((nb, 8, qk_width // heads), F32),
                        pltpu.VMEM((nb, 8, LANE), F32),
                        pltpu.VMEM((nb, 8, qk2), F32)],
        compiler_params=_params(2),
    )(p, *small)


def _out_proj_kernel(h_ref, ya_ref, yb_ref, yc_ref, wa_ref, wb_ref, wc_ref, o_ref):
    acc = jnp.dot(ya_ref[...], wa_ref[...], preferred_element_type=F32)
    acc += jnp.dot(yb_ref[...], wb_ref[...], preferred_element_type=F32)
    acc += jnp.dot(yc_ref[...], wc_ref[...], preferred_element_type=F32)
    o_ref[...] = h_ref[...] + acc


def _out_proj(h, ya, yb, yc, wa, wb, wc, tm):
    m, d = h.shape
    row = lambda arr: pl.BlockSpec((tm, arr.shape[1]), lambda i: (i, 0))
    full = lambda arr: pl.BlockSpec(arr.shape, lambda i: (0, 0))
    return pl.pallas_call(
        _out_proj_kernel,
        grid=(m // tm,),
        in_specs=[row(h), row(ya), row(yb), row(yc), full(wa), full(wb), full(wc)],
        out_specs=pl.BlockSpec((tm, d), lambda i: (i, 0)),
        out_shape=jax.ShapeDtypeStruct((m, d), F32),
        compiler_params=_params(1),
    )(h, ya, yb, yc, wa, wb, wc)


def _ffn_kernel(h_ref, g_ref, w1_ref, w3_ref, w2_ref, gf_ref, o_ref, u_ref, *, final_norm):
    f = pl.program_id(1)

    @pl.when(f == 0)
    def _():
        x = h_ref[...]
        ms = jnp.mean(x * x, axis=-1, keepdims=True)
        u_ref[...] = (x * lax.rsqrt(ms + EPS) * g_ref[...]).astype(BF16)
        o_ref[...] = x

    u = u_ref[...]
    a = jnp.dot(u, w1_ref[...], preferred_element_type=F32)
    b = jnp.dot(u, w3_ref[...], preferred_element_type=F32)
    act = (a * jax.nn.sigmoid(a) * b).astype(BF16)
    o_ref[...] += jnp.dot(act, w2_ref[...], preferred_element_type=F32)

    if final_norm:
        @pl.when(f == pl.num_programs(1) - 1)
        def _():
            y = o_ref[...]
            ms = jnp.mean(y * y, axis=-1, keepdims=True)
            o_ref[...] = y * lax.rsqrt(ms + EPS) * gf_ref[...]


def _ffn(h, g, w1, w3, w2, g_final, tm, tf, final_norm):
    m, d = h.shape
    dff = w1.shape[1]
    return pl.pallas_call(
        functools.partial(_ffn_kernel, final_norm=final_norm),
        grid=(m // tm, dff // tf),
        in_specs=[
            pl.BlockSpec((tm, d), lambda i, f: (i, 0)),
            pl.BlockSpec((1, d), lambda i, f: (0, 0)),
            pl.BlockSpec((d, tf), lambda i, f: (0, f)),
            pl.BlockSpec((d, tf), lambda i, f: (0, f)),
            pl.BlockSpec((tf, d), lambda i, f: (f, 0)),
            pl.BlockSpec((1, d), lambda i, f: (0, 0)),
        ],
        out_specs=pl.BlockSpec((tm, d), lambda i, f: (i, 0)),
        out_shape=jax.ShapeDtypeStruct((m, d), F32),
        scratch_shapes=[pltpu.VMEM((tm, d), BF16)],
        compiler_params=_params(2),
    )(h, g, w1, w3, w2, g_final)


def _pad_groups(w, sizes, axis=-1):
    pieces = []
    off = 0
    for s in sizes:
        piece = lax.slice_in_dim(w, off, off + s, axis=axis)
        pad = _rup(s, LANE) - s
        if pad:
            cfg = [(0, 0)] * w.ndim
            cfg[axis] = (0, pad)
            piece = jnp.pad(piece, cfg)
        pieces.append(piece)
        off += s
    return jnp.concatenate(pieces, axis=axis)


def _tile_rows(m):
    for t in (640, 512, 320, 256, 128, 64, 32, 16, 8):
        if m % t == 0:
            return t
    return m


def _proj_tiles(m, d, n, in_bytes):
    budget = VMEM_LIMIT * 3 // 4
    col_tiles = [t for t in range(n, 0, -LANE) if n % t == 0]
    row_tiles = [t for t in (640, 320, 256, 128, 64, 32, 16, 8) if m % t == 0] or [m]
    for tn in col_tiles:
        for tm in row_tiles:
            need = 2 * tm * d * in_bytes + 2 * d * tn * 2 + 2 * tm * tn * 4
            if need <= budget:
                return tm, tn
    return row_tiles[-1], col_tiles[-1]


def _tile_cols(n, cap):
    best = LANE
    for t in range(LANE, cap + 1, LANE):
        if n % t == 0:
            best = t
    return best


def kernel(x, meta_tokens, norm_mix, w_in, rw_mu, rw_w0, rw_w2, rw_a0, rw_a2, rw_g2, rw_kk, rw_ka, rw_rk, rw_ln_w, rw_ln_b, gla_a2, gla_ab, gla_norm, ml_conv_w, ml_conv_b, ml_ib, ml_fb, ml_norm, w_out, norm_ffn, ffn_w1, ffn_w3, ffn_w2, norm_final):
    bsz, seq, d = x.shape
    depth = w_in.shape[0]
    n_meta = meta_tokens.shape[0]
    ltot = n_meta + seq
    lp = _rup(ltot, CHUNK)

    a_width = rw_w0.shape[-1]
    d_rank, a_rank, g_rank = rw_w2.shape[1], rw_a2.shape[1], rw_g2.shape[1]
    a_sizes = [a_width, a_width, a_width, d_rank, a_rank, g_rank]
    b_rank, b_qk = gla_a2.shape[1], gla_a2.shape[2]
    b_heads = b_qk // (gla_norm.shape[-1] // 2)
    b_width = b_heads * gla_norm.shape[-1]
    b_sizes = [b_qk, b_qk, b_width, b_rank, b_width]
    c_heads = ml_ib.shape[-1]
    c_qk = ml_conv_w.shape[-1] // 2
    c_width = ml_norm.shape[-1]
    c_sizes = [c_qk, c_qk, c_width, 2 * c_heads, c_width]
    a_cols, b_cols = sum(a_sizes), sum(b_sizes)

    meta = jnp.broadcast_to(meta_tokens[None].astype(x.dtype), (bsz, n_meta, d))
    h = jnp.concatenate([meta, x, jnp.zeros((bsz, lp - ltot, d), x.dtype)], axis=1)
    h = h.reshape(bsz * lp, d)
    m = bsz * lp
    tm = _tile_rows(m)
    row2 = lambda t: t.reshape(1, -1)

    w_in_bf, w_out_bf = w_in.astype(BF16), w_out.astype(BF16)
    w1_bf, w3_bf, w2_bf = ffn_w1.astype(BF16), ffn_w3.astype(BF16), ffn_w2.astype(BF16)
    for l in range(depth):
        w_l = w_in_bf[l]
        wa = _pad_groups(w_l[:, :a_cols], a_sizes)
        wb = _pad_groups(w_l[:, a_cols:a_cols + b_cols], b_sizes)
        wc = _pad_groups(w_l[:, a_cols + b_cols:], c_sizes)
        p_a, u = _norm_matmul(h, row2(norm_mix[l]), wa, *_proj_tiles(m, d, wa.shape[1], 6))
        p_b = _matmul(u, wb, *_proj_tiles(m, d, wb.shape[1], 2))
        p_c = _matmul(u, wc, *_proj_tiles(m, d, wc.shape[1], 2))

        y_a = _rwkv_mixer(
            p_a.reshape(bsz, lp, -1), _pad_groups(row2(rw_mu[l]), a_sizes),
            row2(rw_w0[l]), rw_w2[l], row2(rw_a0[l]), rw_a2[l], rw_g2[l], row2(rw_kk[l]),
            row2(rw_ka[l]), row2(rw_rk[l]), row2(rw_ln_w[l]), row2(rw_ln_b[l]))
        y_b = _gla_mixer(p_b.reshape(bsz, lp, -1), gla_a2[l], row2(gla_ab[l]),
                         row2(gla_norm[l]))
        gate_b = jnp.pad(jnp.concatenate([ml_ib[l], ml_fb[l]]), (0, LANE - 2 * c_heads))
        y_c = _mlstm_mixer(p_c.reshape(bsz, lp, -1), ml_conv_w[l], row2(ml_conv_b[l]),
                           row2(gate_b), row2(ml_norm[l]), c_heads)

        wo = w_out_bf[l]
        h = _out_proj(h, y_a.reshape(m, -1), y_b.reshape(m, -1), y_c.reshape(m, -1),
                      wo[:a_width], wo[a_width:a_width + b_width], wo[a_width + b_width:], tm)
        last = l == depth - 1
        h = _ffn(h, row2(norm_ffn[l]), w1_bf[l], w3_bf[l], w2_bf[l], row2(norm_final), tm,
                 _tile_cols(ffn_w1.shape[-1], 512), last)

    return h.reshape(bsz, lp, d)[:, n_meta:ltot]
```

```python
import functools

import jax
import jax.numpy as jnp
from jax import lax
from jax.experimental import pallas as pl
from jax.experimental.pallas import tpu as pltpu

F32 = jnp.float32
BF16 = jnp.bfloat16

EPS = 1e-6
N_META = 16
CHUNK = 64
A_SEQS_PER_STEP = 4
B_SEQS_PER_STEP = 4
C_SEQS_PER_STEP = 1

A_HEAD_DIM = 64
A_GN_EPS = 64e-5
B_DV = 128
B_GATE_NORMALIZER = 16.0
B_SUB = 16
C_DV = 256
C_CONV = 4
C_GATE_CAP = 15.0

LANE = 128
VMEM_LIMIT = 56 * 1024 * 1024

NN = (((1,), (0,)), ((), ()))
NT = (((1,), (1,)), ((), ()))
TN = (((0,), (0,)), ((), ()))

GATE_PASSES = 3
A_PASSES = 1
A_INV_PASSES = 1
B_PASSES = 1
C_PASSES = 1


def _rup(n, m):
    return -(-n // m) * m


def _split(x, parts):
    out = []
    rem = x
    for i in range(parts):
        hi = rem.astype(BF16)
        out.append(hi)
        if i + 1 < parts:
            rem = rem - hi.astype(F32)
    return out


def _mm_parts(a_parts, b_parts, dims):
    dg = lambda a, b: lax.dot_general(a, b, dims, preferred_element_type=F32)
    acc = dg(a_parts[0], b_parts[0])
    if len(a_parts) > 1:
        acc = acc + dg(a_parts[0], b_parts[1]) + dg(a_parts[1], b_parts[0])
    return acc


def _mm(a, b, dims=NN, passes=1):
    n = 2 if passes == 3 else 1
    return _mm_parts(_split(a, n), _split(b, n), dims)


def _cumsum_rows(x):
    tri = _tril_mask(x.shape[0]).astype(BF16)
    hi, mid, lo = (jnp.dot(tri, part, preferred_element_type=F32) for part in _split(x, 3))
    return hi + (mid + lo)


def _softplus(z):
    return jnp.maximum(z, 0.0) + jnp.log1p(jnp.exp(-jnp.abs(z)))


def _log_sigmoid(z):
    return -_softplus(-z)


def _tril_mask(n, strict=False):
    r = lax.broadcasted_iota(jnp.int32, (n, n), 0)
    c = lax.broadcasted_iota(jnp.int32, (n, n), 1)
    return (r > c) if strict else (r >= c)


def _params(n_axes):
    return pltpu.CompilerParams(
        dimension_semantics=("arbitrary",) * n_axes, vmem_limit_bytes=VMEM_LIMIT)


def _norm_matmul_kernel(h_ref, g_ref, w_ref, o_ref, u_ref):
    @pl.when(pl.program_id(1) == 0)
    def _():
        x = h_ref[...]
        ms = jnp.mean(x * x, axis=-1, keepdims=True)
        u_ref[...] = (x * lax.rsqrt(ms + EPS) * g_ref[...]).astype(BF16)

    o_ref[...] = jnp.dot(u_ref[...], w_ref[...], preferred_element_type=F32)


def _norm_matmul(h, g, w, tm, tn):
    m, d = h.shape
    n = w.shape[1]
    return pl.pallas_call(
        _norm_matmul_kernel,
        grid=(m // tm, n // tn),
        in_specs=[
            pl.BlockSpec((tm, d), lambda i, j: (i, 0)),
            pl.BlockSpec((1, d), lambda i, j: (0, 0)),
            pl.BlockSpec((d, tn), lambda i, j: (0, j)),
        ],
        out_specs=[pl.BlockSpec((tm, tn), lambda i, j: (i, j)),
                   pl.BlockSpec((tm, d), lambda i, j: (i, 0))],
        out_shape=[jax.ShapeDtypeStruct((m, n), F32), jax.ShapeDtypeStruct((m, d), BF16)],
        compiler_params=_params(2),
    )(h, g, w)


def _matmul_kernel(u_ref, w_ref, o_ref):
    o_ref[...] = jnp.dot(u_ref[...], w_ref[...], preferred_element_type=F32)


def _matmul(u, w, tm, tn):
    m, d = u.shape
    n = w.shape[1]
    return pl.pallas_call(
        _matmul_kernel,
        grid=(m // tm, n // tn),
        in_specs=[pl.BlockSpec((tm, d), lambda i, j: (i, 0)),
                  pl.BlockSpec((d, tn), lambda i, j: (0, j))],
        out_specs=pl.BlockSpec((tm, tn), lambda i, j: (i, j)),
        out_shape=jax.ShapeDtypeStruct((m, n), F32),
        compiler_params=_params(2),
    )(u, w)


def _seqs_per_step(bsz, want):
    return want if bsz % want == 0 else 1


def _round_robin(gens):
    gens = list(gens)
    while gens:
        alive = []
        for gen in gens:
            try:
                next(gen)
                alive.append(gen)
            except StopIteration:
                pass
        gens = alive


def _rwkv_kernel(p_ref, *rest, width, d_rank, a_rank, g_rank):
    s_ref, prev_ref = rest[-2:]

    @pl.when(pl.program_id(1) == 0)
    def _():
        s_ref[...] = jnp.zeros_like(s_ref)
        prev_ref[...] = jnp.zeros_like(prev_ref)

    _round_robin(_rwkv_chunk(i, p_ref, *rest, width=width, d_rank=d_rank, a_rank=a_rank,
                             g_rank=g_rank) for i in range(p_ref.shape[0]))


def _rwkv_chunk(i, p_ref, mu_ref, w0_ref, w2_ref, a0_ref, a2_ref, g2_ref, kk_ref, ka_ref,
                rk_ref, lnw_ref, lnb_ref, o_ref, s_ref, prev_ref, *, width, d_rank,
                a_rank, g_rank):
    c = CHUNK
    n = A_HEAD_DIM
    heads = width // n

    p = p_ref[i]
    row = lax.broadcasted_iota(jnp.int32, p.shape, 0)
    shifted = jnp.where(row == 0, prev_ref[i], pltpu.roll(p, 1, axis=0))
    prev_ref[i] = p[c - 1:c, :]
    x = p + mu_ref[...] * (shifted - p)

    off_w = 3 * width
    off_a = off_w + _rup(d_rank, LANE)
    off_g = off_a + _rup(a_rank, LANE)
    r = x[:, 0:width]
    k = x[:, width:2 * width]
    v = x[:, 2 * width:3 * width]
    xw = x[:, off_w:off_w + d_rank]
    xa = x[:, off_a:off_a + a_rank]
    xg = x[:, off_g:off_g + g_rank]

    w_log = -_softplus(-(w0_ref[...] + _mm(jnp.tanh(xw), w2_ref[...], NN, GATE_PASSES))) - 0.5
    ld = -jnp.exp(w_log)
    a = jax.nn.sigmoid(a0_ref[...] + _mm(xa, a2_ref[...], NN, GATE_PASSES))
    g = _mm(jax.nn.sigmoid(xg), g2_ref[...], NN, GATE_PASSES)
    kk = k * kk_ref[...]
    k2 = k * (1.0 + (a - 1.0) * ka_ref[...])

    yield
    cum = _cumsum_rows(ld)
    e_pos = jnp.exp(cum)
    e_neg = jnp.exp(-cum)
    e_prev = jnp.exp(cum - ld)
    e_end = jnp.exp(cum[c - 1:c, :] - cum)

    assert c == n and 2 * n == LANE
    pairs = heads // 2
    ps = [slice(j * LANE, (j + 1) * LANE) for j in range(pairs)]
    js = range(pairs)
    first = lax.broadcasted_iota(jnp.int32, (1, LANE), 1) < n
    rows = lax.broadcasted_iota(jnp.int32, (c, 4 * c), 0)
    cols = lax.broadcasted_iota(jnp.int32, (c, 4 * c), 1) % c
    low4 = rows >= cols
    slow2 = (rows > cols)[:, 0:LANE]
    eye2 = (rows == cols)[:, 0:LANE].astype(F32)

    def bdiag(x):
        return jnp.concatenate([jnp.where(first, x, 0.0), jnp.where(first, 0.0, x)], axis=0)

    same_head = (lax.broadcasted_iota(jnp.int32, (LANE, LANE), 0) // n
                 == lax.broadcasted_iota(jnp.int32, (LANE, LANE), 1) // n).astype(BF16)

    def head_sum(x, parts):
        acc = None
        for part in reversed(_split(x, parts)):
            t = jnp.concatenate([jnp.dot(part[:, sl], same_head, preferred_element_type=F32)
                                 for sl in ps], axis=1)
            acc = t if acc is None else acc + t
        return acc

    yield
    kk = kk / jnp.maximum(jnp.sqrt(head_sum(kk * kk, 2)), 1e-12)
    b = kk * a
    lhs_f = jnp.concatenate([-kk * e_prev, r * e_pos], axis=0)
    rhs_f = jnp.concatenate([b * e_neg, k2 * e_neg], axis=0)
    end_f = jnp.concatenate([b * e_end, k2 * e_end], axis=0)

    yield
    s0 = [s_ref[i, j] for j in js]
    big = [_mm(lhs_f[:, ps[j]],
               jnp.concatenate([bdiag(rhs_f[0:c, ps[j]]), bdiag(rhs_f[c:2 * c, ps[j]]),
                                bdiag(s0[j])], axis=0), NT, A_PASSES) for j in js]

    yield
    a_ab = [jnp.where(slow2, big[j][0:c, 0:LANE], 0.0) for j in js]
    v_bd = [bdiag(v[:, ps[j]]) for j in js]
    rhs_u = [big[j][0:c, 2 * LANE:3 * LANE]
             + _mm(jnp.where(slow2, big[j][0:c, LANE:2 * LANE], 0.0), v_bd[j], NN, A_PASSES)
             for j in js]

    yield
    tinv = [eye2 + a_ab[j] for j in js]
    pw = [_mm(a_ab[j], bdiag(a_ab[j]), NN, A_INV_PASSES) for j in js]
    for _ in range((c - 1).bit_length() - 2):
        yield
        res = [_mm(jnp.concatenate([pw[j], tinv[j]], axis=0), bdiag(pw[j]), NN, A_INV_PASSES)
               for j in js]
        pw = [res[j][0:c] for j in js]
        tinv = [tinv[j] + res[j][c:2 * c] for j in js]
    yield
    tinv = [tinv[j] + _mm(tinv[j], bdiag(pw[j]), NN, A_INV_PASSES) for j in js]
    yield
    u = [_mm(tinv[j], bdiag(rhs_u[j]), NN, A_INV_PASSES) for j in js]
    yield
    y = [big[j][c:2 * c, 2 * LANE:3 * LANE]
         + _mm(jnp.where(low4, big[j][c:2 * c, 0:2 * LANE], 0.0),
               jnp.concatenate([bdiag(u[j]), v_bd[j]], axis=0), NN, A_PASSES) for j in js]
    for j in js:
        uv = jnp.concatenate([u[j], v[:, ps[j]]], axis=0)
        full = _mm(uv, end_f[:, ps[j]], TN, A_PASSES)
        s_ref[i, j] = (s0[j] * e_pos[c - 1:c, ps[j]]
                       + jnp.where(first, full[0:n], full[n:2 * n]))

    yield
    y = jnp.concatenate(y, axis=1)
    sums = head_sum(jnp.concatenate([y, r * k2 * rk_ref[...]], axis=0), 2)
    yc = y - sums[0:c] * (1.0 / n)
    var = head_sum(yc * yc, 2) * (1.0 / n)
    yn = yc * lax.rsqrt(var + A_GN_EPS) * lnw_ref[...] + lnb_ref[...]
    bonus = sums[c:2 * c] * v
    o_ref[i] = ((yn + bonus) * g).astype(o_ref.dtype)


def _rwkv_mixer(p, mu, w0, w2, a0, a2, g2, k_k, k_a, r_k, ln_w, ln_b, out_dtype=BF16):
    bsz, lp, cols = p.shape
    nb = _seqs_per_step(bsz, A_SEQS_PER_STEP)
    width = w0.shape[-1]
    heads = width // A_HEAD_DIM
    kern = functools.partial(_rwkv_kernel, width=width, d_rank=w2.shape[0],
                             a_rank=a2.shape[0], g_rank=g2.shape[0])
    full = lambda arr: pl.BlockSpec(arr.shape, lambda b, c: (0,) * arr.ndim)
    small = (mu, w0, w2, a0, a2, g2, k_k, k_a, r_k, ln_w, ln_b)
    return pl.pallas_call(
        kern,
        grid=(bsz // nb, lp // CHUNK),
        in_specs=[pl.BlockSpec((nb, CHUNK, cols), lambda b, c: (b, c, 0))]
        + [full(t) for t in small],
        out_specs=pl.BlockSpec((nb, CHUNK, width), lambda b, c: (b, c, 0)),
        out_shape=jax.ShapeDtypeStruct((bsz, lp, width), out_dtype),
        scratch_shapes=[pltpu.VMEM((nb, heads // 2, A_HEAD_DIM, 2 * A_HEAD_DIM), F32),
                        pltpu.VMEM((nb, 1, cols), F32)],
        compiler_params=_params(2),
    )(p, *small)


def _gla_kernel(p_ref, *rest, qk_width, v_width, rank):
    s_ref = rest[-1]

    @pl.when(pl.program_id(1) == 0)
    def _():
        s_ref[...] = jnp.zeros_like(s_ref)

    _round_robin(_gla_chunk(i, p_ref, *rest, qk_width=qk_width, v_width=v_width, rank=rank)
                 for i in range(p_ref.shape[0]))


def _gla_chunk(i, p_ref, a2_ref, ab_ref, nw_ref, o_ref, s_ref, *, qk_width, v_width, rank):
    c = CHUNK
    dv = B_DV
    heads = v_width // dv
    dk = qk_width // heads

    off_v = 2 * qk_width
    off_a = off_v + v_width
    off_g = off_a + _rup(rank, LANE)
    p = p_ref[i]
    q = p[:, 0:qk_width] * (dk ** -0.5)
    k = p[:, qk_width:off_v]
    v = p[:, off_v:off_a]
    xa = p[:, off_a:off_a + rank]
    gate = p[:, off_g:off_g + v_width]

    lg = _log_sigmoid(_mm(xa, a2_ref[...], NN, GATE_PASSES) + ab_ref[...]) / B_GATE_NORMALIZER
    yield
    bcum = _cumsum_rows(lg)
    low = _tril_mask(c)
    b_last = bcum[c - 1:c, :]
    q_in = q * jnp.exp(bcum)
    k_end = k * jnp.exp(b_last - bcum)
    e_last = jnp.exp(b_last)

    qf, kf = [], []
    for i0 in range(0, c, B_SUB):
        i1 = i0 + B_SUB
        ref_pt = bcum[i1 - 1:i1, :]
        qf.append(q[i0:i1] * jnp.exp(bcum[i0:i1] - ref_pt))
        kf.append(k[0:i1] * jnp.exp(ref_pt - bcum[0:i1]))

    hs = range(heads)
    sls = [slice(h * dk, (h + 1) * dk) for h in hs]
    vss = [slice(h * dv, (h + 1) * dv) for h in hs]
    yield
    st = [s_ref[i, h] for h in hs]
    o = [_mm(q_in[:, sls[h]], st[h], NT, B_PASSES) for h in hs]
    starts = list(range(0, c, B_SUB))
    sc = [[_mm(qf[bi][:, sls[h]], kf[bi][:, sls[h]], NT, B_PASSES)
           for bi in range(len(starts))] for h in hs]
    yield
    sc = [[jnp.where(low[i0:i0 + B_SUB, 0:i0 + B_SUB], sc[h][bi], 0.0)
           for bi, i0 in enumerate(starts)] for h in hs]
    for h in hs:
        blocks = [_mm(sc[h][bi], v[0:i0 + B_SUB, vss[h]], NN, B_PASSES)
                  for bi, i0 in enumerate(starts)]
        o[h] = o[h] + jnp.concatenate(blocks, axis=0)
    for h in hs:
        s_ref[i, h] = st[h] * e_last[:, sls[h]] + _mm(v[:, vss[h]], k_end[:, sls[h]], TN, B_PASSES)
    yield
    for h in hs:
        vs = vss[h]
        o_h = o[h] * lax.rsqrt(jnp.mean(o[h] * o[h], axis=-1, keepdims=True) + EPS) * nw_ref[...]
        g_h = gate[:, vs]
        o_ref[i, :, vs] = (o_h * (g_h * jax.nn.sigmoid(g_h))).astype(o_ref.dtype)


def _gla_mixer(p, a2, ab, norm_w, out_dtype=BF16):
    bsz, lp, cols = p.shape
    rank, qk_width = a2.shape
    dv = norm_w.shape[-1]
    heads = qk_width // (dv // 2)
    v_width = heads * dv
    nb = _seqs_per_step(bsz, B_SEQS_PER_STEP)
    kern = functools.partial(_gla_kernel, qk_width=qk_width, v_width=v_width, rank=rank)
    full = lambda arr: pl.BlockSpec(arr.shape, lambda b, c: (0,) * arr.ndim)
    small = (a2, ab, norm_w)
    return pl.pallas_call(
        kern,
        grid=(bsz // nb, lp // CHUNK),
        in_specs=[pl.BlockSpec((nb, CHUNK, cols), lambda b, c: (b, c, 0))]
        + [full(t) for t in small],
        out_specs=pl.BlockSpec((nb, CHUNK, v_width), lambda b, c: (b, c, 0)),
        out_shape=jax.ShapeDtypeStruct((bsz, lp, v_width), out_dtype),
        scratch_shapes=[pltpu.VMEM((nb, heads, dv, qk_width // heads), F32)],
        compiler_params=_params(2),
    )(p, *small)


def _mlstm_kernel(p_ref, *rest, qk_width, v_width, heads):
    @pl.when(pl.program_id(1) == 0)
    def _():
        for ref in rest[-4:]:
            ref[...] = jnp.zeros_like(ref)

    _round_robin(_mlstm_chunk(i, p_ref, *rest, qk_width=qk_width, v_width=v_width,
                              heads=heads) for i in range(p_ref.shape[0]))


def _mlstm_chunk(i, p_ref, cw_ref, cb_ref, gb_ref, nw_ref, o_ref, cm_ref, n_ref, m_ref,
                 hist_ref, *, qk_width, v_width, heads):
    c = CHUNK
    dqk = qk_width // heads
    dv = v_width // heads

    off_v = 2 * qk_width
    off_g = off_v + v_width
    off_o = off_g + LANE
    p = p_ref[i]
    qk_raw = p[:, 0:off_v]
    v = p[:, off_v:off_g]
    gates = p[:, off_g:off_o]
    og = p[:, off_o:off_o + v_width]

    ext = jnp.concatenate([hist_ref[i], qk_raw], axis=0)
    hist_ref[i] = qk_raw[c - 8:c, :]
    conv = cb_ref[...] + ext * cw_ref[C_CONV - 1:C_CONV, :]
    for d in range(1, C_CONV):
        conv = conv + pltpu.roll(ext, d, axis=0) * cw_ref[C_CONV - 1 - d:C_CONV - d, :]
    conv = conv[8:, :]
    qk = conv * jax.nn.sigmoid(conv)
    q = qk[:, 0:qk_width]
    k = qk[:, qk_width:off_v] * (dqk ** -0.5)

    capped = C_GATE_CAP * jnp.tanh((gates + gb_ref[...]) / C_GATE_CAP)
    lf = _log_sigmoid(capped)
    yield
    bcol = _cumsum_rows(lf)
    brow = bcol.T
    irow = capped.T
    low = _tril_mask(c)
    hs = range(heads)
    qss = [slice(h * dqk, (h + 1) * dqk) for h in hs]
    vss = [slice(h * dv, (h + 1) * dv) for h in hs]
    np_ = 2 if C_PASSES == 3 else 1

    q_p = [_split(q[:, qss[h]], np_) for h in hs]
    k_p = [_split(k[:, qss[h]], np_) for h in hs]
    yield
    cm = [cm_ref[i, h] for h in hs]
    qk_s = [_mm_parts(q_p[h], k_p[h], NT) for h in hs]
    q_cm = [_mm_parts(q_p[h], _split(cm[h], np_), NN) for h in hs]

    yield
    bc = [bcol[:, heads + h:heads + h + 1] for h in hs]
    ic = [capped[:, h:h + 1] for h in hs]
    m_prev = [m_ref[i, h:h + 1, 0:1] for h in hs]
    n_row = [n_ref[i, h:h + 1, :] for h in hs]
    log_w = [jnp.where(low, bc[h] - brow[heads + h:heads + h + 1, :] + irow[h:h + 1, :],
                       -jnp.inf) for h in hs]
    log_prev = [bc[h] + m_prev[h] for h in hs]
    m_t = [jnp.maximum(log_prev[h], jnp.max(log_w[h], axis=-1, keepdims=True)) for h in hs]
    w_prev = [jnp.exp(log_prev[h] - m_t[h]) for h in hs]
    s = [qk_s[h] * jnp.exp(log_w[h] - m_t[h]) for h in hs]
    yield
    num = [w_prev[h] * q_cm[h] + _mm(s[h], v[:, vss[h]], NN, C_PASSES) for h in hs]
    den = [w_prev[h] * jnp.sum(q[:, qss[h]] * n_row[h], axis=-1, keepdims=True)
           + jnp.sum(s[h], axis=-1, keepdims=True) for h in hs]
    hh = [num[h] / jnp.maximum(jnp.abs(den[h]), jnp.exp(-m_t[h])) for h in hs]

    yield
    m_new = [m_t[h][c - 1:c, :] for h in hs]
    b_last = [bc[h][c - 1:c, :] for h in hs]
    kw = [k[:, qss[h]] * jnp.exp(b_last[h] - bc[h] + ic[h] - m_new[h]) for h in hs]
    f_end = [jnp.exp(b_last[h] + m_prev[h] - m_new[h]) for h in hs]
    for h in hs:
        cm_ref[i, h] = f_end[h] * cm[h] + _mm(kw[h], v[:, vss[h]], TN, C_PASSES)
        n_ref[i, h:h + 1, :] = f_end[h] * n_row[h] + jnp.sum(kw[h], axis=0, keepdims=True)
        m_ref[i, h:h + 1, :] = jnp.broadcast_to(m_new[h], (1, m_ref.shape[-1]))

    yield
    for h in hs:
        vs = vss[h]
        hn = hh[h] * lax.rsqrt(jnp.mean(hh[h] * hh[h], axis=-1, keepdims=True) + EPS)
        o_ref[i, :, vs] = (hn * nw_ref[:, vs] * jax.nn.sigmoid(og[:, vs])).astype(o_ref.dtype)


def _mlstm_mixer(p, conv_w, conv_b, gate_b, norm_w, heads, out_dtype=BF16):
    bsz, lp, cols = p.shape
    qk2 = conv_w.shape[-1]
    qk_width = qk2 // 2
    v_width = norm_w.shape[-1]
    nb = _seqs_per_step(bsz, C_SEQS_PER_STEP)
    kern = functools.partial(_mlstm_kernel, qk_width=qk_width, v_width=v_width, heads=heads)
    full = lambda arr: pl.BlockSpec(arr.shape, lambda b, c: (0,) * arr.ndim)
    small = (conv_w, conv_b, gate_b, norm_w)
    return pl.pallas_call(
        kern,
        grid=(bsz // nb, lp // CHUNK),
        in_specs=[pl.BlockSpec((nb, CHUNK, cols), lambda b, c: (b, c, 0))]
        + [full(t) for t in small],
        out_specs=pl.BlockSpec((nb, CHUNK, v_width), lambda b, c: (b, c, 0)),
        out_shape=jax.ShapeDtypeStruct((bsz, lp, v_width), out_dtype),
        scratch_shapes=[pltpu.VMEM((nb, heads, qk_width // heads, v_width // heads), F32),
                        pltpu.VMEM((nb, 8, qk_width // heads), F32),
                        pltpu.VMEM((nb, 8, LANE), F32),
                        pltpu.VMEM((nb, 8, qk2), F32)],
        compiler_params=_params(2),
    )(p, *small)


def _out_proj_kernel(h_ref, ya_ref, yb_ref, yc_ref, wa_ref, wb_ref, wc_ref, o_ref):
    acc = jnp.dot(ya_ref[...], wa_ref[...], preferred_element_type=F32)
    acc += jnp.dot(yb_ref[...], wb_ref[...], preferred_element_type=F32)
    acc += jnp.dot(yc_ref[...], wc_ref[...], preferred_element_type=F32)
    o_ref[...] = h_ref[...] + acc


def _out_proj(h, ya, yb, yc, wa, wb, wc, tm):
    m, d = h.shape
    row = lambda arr: pl.BlockSpec((tm, arr.shape[1]), lambda i: (i, 0))
    full = lambda arr: pl.BlockSpec(arr.shape, lambda i: (0, 0))
    return pl.pallas_call(
        _out_proj_kernel,
        grid=(m // tm,),
        in_specs=[row(h), row(ya), row(yb), row(yc), full(wa), full(wb), full(wc)],
        out_specs=pl.BlockSpec((tm, d), lambda i: (i, 0)),
        out_shape=jax.ShapeDtypeStruct((m, d), F32),
        compiler_params=_params(1),
    )(h, ya, yb, yc, wa, wb, wc)


def _ffn_kernel(h_ref, g_ref, w1_ref, w3_ref, w2_ref, gf_ref, o_ref, u_ref, *, final_norm,
                f_axis):
    f = pl.program_id(f_axis)

    @pl.when(f == 0)
    def _():
        x = h_ref[...]
        ms = jnp.mean(x * x, axis=-1, keepdims=True)
        u_ref[...] = (x * lax.rsqrt(ms + EPS) * g_ref[...]).astype(BF16)
        o_ref[...] = x

    u = u_ref[...]
    a = jnp.dot(u, w1_ref[...], preferred_element_type=F32)
    b = jnp.dot(u, w3_ref[...], preferred_element_type=F32)
    act = (a * jax.nn.sigmoid(a) * b).astype(BF16)
    o_ref[...] += jnp.dot(act, w2_ref[...], preferred_element_type=F32)

    if final_norm:
        @pl.when(f == pl.num_programs(f_axis) - 1)
        def _():
            y = o_ref[...]
            ms = jnp.mean(y * y, axis=-1, keepdims=True)
            o_ref[...] = y * lax.rsqrt(ms + EPS) * gf_ref[...]


def _ffn(h, g, w1, w3, w2, g_final, tm, tf, final_norm):
    m, d = h.shape
    dff = w1.shape[1]
    return pl.pallas_call(
        functools.partial(_ffn_kernel, final_norm=final_norm, f_axis=1),
        grid=(m // tm, dff // tf),
        in_specs=[
            pl.BlockSpec((tm, d), lambda i, f: (i, 0)),
            pl.BlockSpec((1, d), lambda i, f: (0, 0)),
            pl.BlockSpec((d, tf), lambda i, f: (0, f)),
            pl.BlockSpec((d, tf), lambda i, f: (0, f)),
            pl.BlockSpec((tf, d), lambda i, f: (f, 0)),
            pl.BlockSpec((1, d), lambda i, f: (0, 0)),
        ],
        out_specs=pl.BlockSpec((tm, d), lambda i, f: (i, 0)),
        out_shape=jax.ShapeDtypeStruct((m, d), F32),
        scratch_shapes=[pltpu.VMEM((tm, d), BF16)],
        compiler_params=_params(2),
    )(h, g, w1, w3, w2, g_final)


def _ffn_last(h3, g, w1, w3, w2, g_final, row0, rows, tm, tf):
    bsz, lp, d = h3.shape
    dff = w1.shape[1]
    return pl.pallas_call(
        functools.partial(_ffn_kernel, final_norm=True, f_axis=2),
        grid=(bsz, rows // tm, dff // tf),
        in_specs=[
            pl.BlockSpec((pl.Element(tm), pl.Element(d)),
                         lambda b, i, f: (pl.multiple_of(b * lp + row0 + i * tm, 8), 0)),
            pl.BlockSpec((1, d), lambda b, i, f: (0, 0)),
            pl.BlockSpec((d, tf), lambda b, i, f: (0, f)),
            pl.BlockSpec((d, tf), lambda b, i, f: (0, f)),
            pl.BlockSpec((tf, d), lambda b, i, f: (f, 0)),
            pl.BlockSpec((1, d), lambda b, i, f: (0, 0)),
        ],
        out_specs=pl.BlockSpec((None, tm, d), lambda b, i, f: (b, i, 0)),
        out_shape=jax.ShapeDtypeStruct((bsz, rows, d), F32),
        scratch_shapes=[pltpu.VMEM((tm, d), BF16)],
        compiler_params=_params(3),
    )(h3.reshape(bsz * lp, d), g, w1, w3, w2, g_final)


def _pad_groups(w, sizes, axis=-1):
    pieces = []
    off = 0
    for s in sizes:
        piece = lax.slice_in_dim(w, off, off + s, axis=axis)
        pad = _rup(s, LANE) - s
        if pad:
            cfg = [(0, 0)] * w.ndim
            cfg[axis] = (0, pad)
            piece = jnp.pad(piece, cfg)
        pieces.append(piece)
        off += s
    return jnp.concatenate(pieces, axis=axis)


def _tile_rows(m):
    for t in (640, 512, 320, 256, 128, 64, 32, 16, 8):
        if m % t == 0:
            return t
    return m


def _proj_tiles(m, d, n, in_bytes):
    budget = VMEM_LIMIT * 3 // 4
    col_tiles = [t for t in range(n, 0, -LANE) if n % t == 0]
    row_tiles = [t for t in (640, 320, 256, 128, 64, 32, 16, 8) if m % t == 0] or [m]
    for tn in col_tiles:
        for tm in row_tiles:
            need = 2 * tm * d * in_bytes + 2 * d * tn * 2 + 2 * tm * tn * 4
            if need <= budget:
                return tm, tn
    return row_tiles[-1], col_tiles[-1]


def _tile_cols(n, cap):
    best = LANE
    for t in range(LANE, cap + 1, LANE):
        if n % t == 0:
            best = t
    return best


def kernel(x, meta_tokens, norm_mix, w_in, rw_mu, rw_w0, rw_w2, rw_a0, rw_a2, rw_g2, rw_kk, rw_ka, rw_rk, rw_ln_w, rw_ln_b, gla_a2, gla_ab, gla_norm, ml_conv_w, ml_conv_b, ml_ib, ml_fb, ml_norm, w_out, norm_ffn, ffn_w1, ffn_w3, ffn_w2, norm_final):
    bsz, seq, d = x.shape
    depth = w_in.shape[0]
    n_meta = meta_tokens.shape[0]
    ltot = n_meta + seq
    lp = _rup(ltot, CHUNK)

    a_width = rw_w0.shape[-1]
    d_rank, a_rank, g_rank = rw_w2.shape[1], rw_a2.shape[1], rw_g2.shape[1]
    a_sizes = [a_width, a_width, a_width, d_rank, a_rank, g_rank]
    b_rank, b_qk = gla_a2.shape[1], gla_a2.shape[2]
    b_heads = b_qk // (gla_norm.shape[-1] // 2)
    b_width = b_heads * gla_norm.shape[-1]
    b_sizes = [b_qk, b_qk, b_width, b_rank, b_width]
    c_heads = ml_ib.shape[-1]
    c_qk = ml_conv_w.shape[-1] // 2
    c_width = ml_norm.shape[-1]
    c_sizes = [c_qk, c_qk, c_width, 2 * c_heads, c_width]
    a_cols, b_cols = sum(a_sizes), sum(b_sizes)

    meta = jnp.broadcast_to(meta_tokens[None].astype(x.dtype), (bsz, n_meta, d))
    h = jnp.concatenate([meta, x, jnp.zeros((bsz, lp - ltot, d), x.dtype)], axis=1)
    h = h.reshape(bsz * lp, d)
    m = bsz * lp
    tm = _tile_rows(m)
    row2 = lambda t: t.reshape(1, -1)

    w_out_bf = w_out.astype(BF16)
    w1_bf, w3_bf, w2_bf = ffn_w1.astype(BF16), ffn_w3.astype(BF16), ffn_w2.astype(BF16)
    for l in range(depth):
        w_l = w_in[l]
        wa = _pad_groups(w_l[:, :a_cols], a_sizes).astype(BF16)
        wb = _pad_groups(w_l[:, a_cols:a_cols + b_cols], b_sizes).astype(BF16)
        wc = _pad_groups(w_l[:, a_cols + b_cols:], c_sizes).astype(BF16)
        p_a, u = _norm_matmul(h, row2(norm_mix[l]), wa, *_proj_tiles(m, d, wa.shape[1], 6))
        p_b = _matmul(u, wb, *_proj_tiles(m, d, wb.shape[1], 2))
        p_c = _matmul(u, wc, *_proj_tiles(m, d, wc.shape[1], 2))

        y_a = _rwkv_mixer(
            p_a.reshape(bsz, lp, -1), _pad_groups(row2(rw_mu[l]), a_sizes),
            row2(rw_w0[l]), rw_w2[l], row2(rw_a0[l]), rw_a2[l], rw_g2[l], row2(rw_kk[l]),
            row2(rw_ka[l]), row2(rw_rk[l]), row2(rw_ln_w[l]), row2(rw_ln_b[l]))
        y_b = _gla_mixer(p_b.reshape(bsz, lp, -1), gla_a2[l], row2(gla_ab[l]),
                         row2(gla_norm[l]))
        gate_b = jnp.pad(jnp.concatenate([ml_ib[l], ml_fb[l]]), (0, LANE - 2 * c_heads))
        y_c = _mlstm_mixer(p_c.reshape(bsz, lp, -1), ml_conv_w[l], row2(ml_conv_b[l]),
                           row2(gate_b), row2(ml_norm[l]), c_heads)

        wo = w_out_bf[l]
        h = _out_proj(h, y_a.reshape(m, -1), y_b.reshape(m, -1), y_c.reshape(m, -1),
                      wo[:a_width], wo[a_width:a_width + b_width], wo[a_width + b_width:], tm)
        tf = _tile_cols(ffn_w1.shape[-1], 512)
        ffn_args = (row2(norm_ffn[l]), w1_bf[l], w3_bf[l], w2_bf[l], row2(norm_final))
        if l + 1 < depth:
            h = _ffn(h, *ffn_args, tm, tf, False)
        else:
            out = _ffn_last(h.reshape(bsz, lp, d), *ffn_args, n_meta, seq,
                            _tile_cols(seq, 512), tf)
    return out
```

```python
import functools

import jax
import jax.numpy as jnp
from jax import lax
from jax.experimental import pallas as pl
from jax.experimental.pallas import tpu as pltpu

F32 = jnp.float32
BF16 = jnp.bfloat16

EPS = 1e-6
CHUNK = 64
A_SEQS_PER_STEP = 4
B_SEQS_PER_STEP = 4
C_SEQS_PER_STEP = 1

A_HEAD_DIM = 64
A_GN_EPS = 64e-5
B_DV = 128
B_GATE_NORMALIZER = 16.0
B_SUB = 16
C_CONV = 4
C_GATE_CAP = 15.0

LANE = 128
VMEM_LIMIT = 56 * 1024 * 1024

NN = (((1,), (0,)), ((), ()))
NT = (((1,), (1,)), ((), ()))
TN = (((0,), (0,)), ((), ()))

GATE_PASSES = 3
A_PASSES = 1
A_INV_PASSES = 1
B_PASSES = 1
C_PASSES = 1


def _rup(n, m):
    return -(-n // m) * m


def _split(x, parts):
    out = []
    rem = x
    for i in range(parts):
        hi = rem.astype(BF16)
        out.append(hi)
        if i + 1 < parts:
            rem = rem - hi.astype(F32)
    return out


def _mm_parts(a_parts, b_parts, dims):
    dg = lambda a, b: lax.dot_general(a, b, dims, preferred_element_type=F32)
    acc = dg(a_parts[0], b_parts[0])
    if len(a_parts) > 1:
        acc = acc + dg(a_parts[0], b_parts[1]) + dg(a_parts[1], b_parts[0])
    return acc


def _mm(a, b, dims=NN, passes=1):
    n = 2 if passes == 3 else 1
    return _mm_parts(_split(a, n), _split(b, n), dims)


def _cumsum_rows(x):
    tri = _tril_mask(x.shape[0]).astype(BF16)
    hi, mid, lo = (jnp.dot(tri, part, preferred_element_type=F32) for part in _split(x, 3))
    return hi + (mid + lo)


def _softplus(z):
    return jnp.maximum(z, 0.0) + jnp.log1p(jnp.exp(-jnp.abs(z)))


def _log_sigmoid(z):
    return -_softplus(-z)


def _tril_mask(n, strict=False):
    r = lax.broadcasted_iota(jnp.int32, (n, n), 0)
    c = lax.broadcasted_iota(jnp.int32, (n, n), 1)
    return (r > c) if strict else (r >= c)


def _params(n_axes):
    return pltpu.CompilerParams(
        dimension_semantics=("arbitrary",) * n_axes, vmem_limit_bytes=VMEM_LIMIT)


def _norm_matmul_kernel(h_ref, g_ref, w_ref, o_ref, u_ref):
    @pl.when(pl.program_id(1) == 0)
    def _():
        x = h_ref[...]
        ms = jnp.mean(x * x, axis=-1, keepdims=True)
        u_ref[...] = (x * lax.rsqrt(ms + EPS) * g_ref[...]).astype(BF16)

    o_ref[...] = jnp.dot(u_ref[...], w_ref[...], preferred_element_type=F32)


def _norm_matmul(h, g, w, tm, tn):
    m, d = h.shape
    n = w.shape[1]
    return pl.pallas_call(
        _norm_matmul_kernel,
        grid=(m // tm, n // tn),
        in_specs=[
            pl.BlockSpec((tm, d), lambda i, j: (i, 0)),
            pl.BlockSpec((1, d), lambda i, j: (0, 0)),
            pl.BlockSpec((d, tn), lambda i, j: (0, j)),
        ],
        out_specs=[pl.BlockSpec((tm, tn), lambda i, j: (i, j)),
                   pl.BlockSpec((tm, d), lambda i, j: (i, 0))],
        out_shape=[jax.ShapeDtypeStruct((m, n), F32), jax.ShapeDtypeStruct((m, d), BF16)],
        compiler_params=_params(2),
    )(h, g, w)


def _matmul_kernel(u_ref, w_ref, o_ref):
    o_ref[...] = jnp.dot(u_ref[...], w_ref[...], preferred_element_type=F32)


def _matmul(u, w, tm, tn):
    m, d = u.shape
    n = w.shape[1]
    return pl.pallas_call(
        _matmul_kernel,
        grid=(m // tm, n // tn),
        in_specs=[pl.BlockSpec((tm, d), lambda i, j: (i, 0)),
                  pl.BlockSpec((d, tn), lambda i, j: (0, j))],
        out_specs=pl.BlockSpec((tm, tn), lambda i, j: (i, j)),
        out_shape=jax.ShapeDtypeStruct((m, n), F32),
        compiler_params=_params(2),
    )(u, w)


def _seqs_per_step(bsz, want):
    return want if bsz % want == 0 else 1


def _round_robin(gens):
    gens = list(gens)
    while gens:
        alive = []
        for gen in gens:
            try:
                next(gen)
                alive.append(gen)
            except StopIteration:
                pass
        gens = alive


def _rwkv_kernel(p_ref, *rest, width, d_rank, a_rank, g_rank):
    s_ref, prev_ref = rest[-2:]

    @pl.when(pl.program_id(1) == 0)
    def _():
        s_ref[...] = jnp.zeros_like(s_ref)
        prev_ref[...] = jnp.zeros_like(prev_ref)

    _round_robin(_rwkv_chunk(i, p_ref, *rest, width=width, d_rank=d_rank, a_rank=a_rank,
                             g_rank=g_rank) for i in range(p_ref.shape[0]))


def _rwkv_chunk(i, p_ref, mu_ref, w0_ref, w2_ref, a0_ref, a2_ref, g2_ref, kk_ref, ka_ref,
                rk_ref, lnw_ref, lnb_ref, o_ref, s_ref, prev_ref, *, width, d_rank,
                a_rank, g_rank):
    c = CHUNK
    n = A_HEAD_DIM
    heads = width // n

    p = p_ref[i]
    row = lax.broadcasted_iota(jnp.int32, p.shape, 0)
    shifted = jnp.where(row == 0, prev_ref[i], pltpu.roll(p, 1, axis=0))
    prev_ref[i] = p[c - 1:c, :]
    x = p + mu_ref[...] * (shifted - p)

    off_w = 3 * width
    off_a = off_w + _rup(d_rank, LANE)
    off_g = off_a + _rup(a_rank, LANE)
    r = x[:, 0:width]
    k = x[:, width:2 * width]
    v = x[:, 2 * width:3 * width]
    xw = x[:, off_w:off_w + d_rank]
    xa = x[:, off_a:off_a + a_rank]
    xg = x[:, off_g:off_g + g_rank]

    w_log = -_softplus(-(w0_ref[...] + _mm(jnp.tanh(xw), w2_ref[...], NN, GATE_PASSES))) - 0.5
    ld = -jnp.exp(w_log)
    a = jax.nn.sigmoid(a0_ref[...] + _mm(xa, a2_ref[...], NN, GATE_PASSES))
    g = _mm(jax.nn.sigmoid(xg), g2_ref[...], NN, GATE_PASSES)
    kk = k * kk_ref[...]
    k2 = k * (1.0 + (a - 1.0) * ka_ref[...])

    yield
    cum = _cumsum_rows(ld)
    e_pos = jnp.exp(cum)
    e_neg = jnp.exp(-cum)
    e_prev = jnp.exp(cum - ld)
    e_end = jnp.exp(cum[c - 1:c, :] - cum)

    assert c == n and 2 * n == LANE
    pairs = heads // 2
    ps = [slice(j * LANE, (j + 1) * LANE) for j in range(pairs)]
    js = range(pairs)
    first = lax.broadcasted_iota(jnp.int32, (1, LANE), 1) < n
    rows = lax.broadcasted_iota(jnp.int32, (c, 4 * c), 0)
    cols = lax.broadcasted_iota(jnp.int32, (c, 4 * c), 1) % c
    low4 = rows >= cols
    slow2 = (rows > cols)[:, 0:LANE]
    eye2 = (rows == cols)[:, 0:LANE].astype(F32)

    def bdiag(x):
        return jnp.concatenate([jnp.where(first, x, 0.0), jnp.where(first, 0.0, x)], axis=0)

    same_head = (lax.broadcasted_iota(jnp.int32, (LANE, LANE), 0) // n
                 == lax.broadcasted_iota(jnp.int32, (LANE, LANE), 1) // n).astype(BF16)

    def head_sum(x, parts):
        acc = None
        for part in reversed(_split(x, parts)):
            t = jnp.concatenate([jnp.dot(part[:, sl], same_head, preferred_element_type=F32)
                                 for sl in ps], axis=1)
            acc = t if acc is None else acc + t
        return acc

    yield
    kk = kk / jnp.maximum(jnp.sqrt(head_sum(kk * kk, 2)), 1e-12)
    b = kk * a
    lhs_f = jnp.concatenate([-kk * e_prev, r * e_pos], axis=0)
    rhs_f = jnp.concatenate([b * e_neg, k2 * e_neg], axis=0)
    end_f = jnp.concatenate([b * e_end, k2 * e_end], axis=0)

    yield
    s0 = [s_ref[i, j] for j in js]
    big = [_mm(lhs_f[:, ps[j]],
               jnp.concatenate([bdiag(rhs_f[0:c, ps[j]]), bdiag(rhs_f[c:2 * c, ps[j]]),
                                bdiag(s0[j])], axis=0), NT, A_PASSES) for j in js]

    yield
    a_ab = [jnp.where(slow2, big[j][0:c, 0:LANE], 0.0) for j in js]
    v_bd = [bdiag(v[:, ps[j]]) for j in js]
    rhs_u = [big[j][0:c, 2 * LANE:3 * LANE]
             + _mm(jnp.where(slow2, big[j][0:c, LANE:2 * LANE], 0.0), v_bd[j], NN, A_PASSES)
             for j in js]

    yield
    tinv = [eye2 + a_ab[j] for j in js]
    pw = [_mm(a_ab[j], bdiag(a_ab[j]), NN, A_INV_PASSES) for j in js]
    for _ in range((c - 1).bit_length() - 2):
        yield
        res = [_mm(jnp.concatenate([pw[j], tinv[j]], axis=0), bdiag(pw[j]), NN, A_INV_PASSES)
               for j in js]
        pw = [res[j][0:c] for j in js]
        tinv = [tinv[j] + res[j][c:2 * c] for j in js]
    yield
    tinv = [tinv[j] + _mm(tinv[j], bdiag(pw[j]), NN, A_INV_PASSES) for j in js]
    yield
    u = [_mm(tinv[j], bdiag(rhs_u[j]), NN, A_INV_PASSES) for j in js]
    yield
    y = [big[j][c:2 * c, 2 * LANE:3 * LANE]
         + _mm(jnp.where(low4, big[j][c:2 * c, 0:2 * LANE], 0.0),
               jnp.concatenate([bdiag(u[j]), v_bd[j]], axis=0), NN, A_PASSES) for j in js]
    for j in js:
        uv = jnp.concatenate([u[j], v[:, ps[j]]], axis=0)
        full = _mm(uv, end_f[:, ps[j]], TN, A_PASSES)
        s_ref[i, j] = (s0[j] * e_pos[c - 1:c, ps[j]]
                       + jnp.where(first, full[0:n], full[n:2 * n]))

    yield
    y = jnp.concatenate(y, axis=1)
    sums = head_sum(jnp.concatenate([y, r * k2 * rk_ref[...]], axis=0), 2)
    yc = y - sums[0:c] * (1.0 / n)
    var = head_sum(yc * yc, 2) * (1.0 / n)
    yn = yc * lax.rsqrt(var + A_GN_EPS) * lnw_ref[...] + lnb_ref[...]
    bonus = sums[c:2 * c] * v
    o_ref[i] = ((yn + bonus) * g).astype(o_ref.dtype)


def _rwkv_mixer(p, mu, w0, w2, a0, a2, g2, k_k, k_a, r_k, ln_w, ln_b, out_dtype=BF16):
    bsz, lp, cols = p.shape
    nb = _seqs_per_step(bsz, A_SEQS_PER_STEP)
    width = w0.shape[-1]
    heads = width // A_HEAD_DIM
    kern = functools.partial(_rwkv_kernel, width=width, d_rank=w2.shape[0],
                             a_rank=a2.shape[0], g_rank=g2.shape[0])
    full = lambda arr: pl.BlockSpec(arr.shape, lambda b, c: (0,) * arr.ndim)
    small = (mu, w0, w2, a0, a2, g2, k_k, k_a, r_k, ln_w, ln_b)
    return pl.pallas_call(
        kern,
        grid=(bsz // nb, lp // CHUNK),
        in_specs=[pl.BlockSpec((nb, CHUNK, cols), lambda b, c: (b, c, 0))]
        + [full(t) for t in small],
        out_specs=pl.BlockSpec((nb, CHUNK, width), lambda b, c: (b, c, 0)),
        out_shape=jax.ShapeDtypeStruct((bsz, lp, width), out_dtype),
        scratch_shapes=[pltpu.VMEM((nb, heads // 2, A_HEAD_DIM, 2 * A_HEAD_DIM), F32),
                        pltpu.VMEM((nb, 1, cols), F32)],
        compiler_params=_params(2),
    )(p, *small)


def _gla_kernel(p_ref, *rest, qk_width, v_width, rank):
    s_ref = rest[-1]

    @pl.when(pl.program_id(1) == 0)
    def _():
        s_ref[...] = jnp.zeros_like(s_ref)

    _round_robin(_gla_chunk(i, p_ref, *rest, qk_width=qk_width, v_width=v_width, rank=rank)
                 for i in range(p_ref.shape[0]))


def _gla_chunk(i, p_ref, a2_ref, ab_ref, nw_ref, o_ref, s_ref, *, qk_width, v_width, rank):
    c = CHUNK
    dv = B_DV
    heads = v_width // dv
    dk = qk_width // heads

    off_v = 2 * qk_width
    off_a = off_v + v_width
    off_g = off_a + _rup(rank, LANE)
    p = p_ref[i]
    q = p[:, 0:qk_width] * (dk ** -0.5)
    k = p[:, qk_width:off_v]
    v = p[:, off_v:off_a]
    xa = p[:, off_a:off_a + rank]
    gate = p[:, off_g:off_g + v_width]

    lg = _log_sigmoid(_mm(xa, a2_ref[...], NN, GATE_PASSES) + ab_ref[...]) / B_GATE_NORMALIZER
    yield
    bcum = _cumsum_rows(lg)
    low = _tril_mask(c)
    b_last = bcum[c - 1:c, :]
    q_in = q * jnp.exp(bcum)
    k_end = k * jnp.exp(b_last - bcum)
    e_last = jnp.exp(b_last)

    qf, kf = [], []
    for i0 in range(0, c, B_SUB):
        i1 = i0 + B_SUB
        ref_pt = bcum[i1 - 1:i1, :]
        qf.append(q[i0:i1] * jnp.exp(bcum[i0:i1] - ref_pt))
        kf.append(k[0:i1] * jnp.exp(ref_pt - bcum[0:i1]))

    hs = range(heads)
    sls = [slice(h * dk, (h + 1) * dk) for h in hs]
    vss = [slice(h * dv, (h + 1) * dv) for h in hs]
    yield
    st = [s_ref[i, h] for h in hs]
    o = [_mm(q_in[:, sls[h]], st[h], NT, B_PASSES) for h in hs]
    starts = list(range(0, c, B_SUB))
    sc = [[_mm(qf[bi][:, sls[h]], kf[bi][:, sls[h]], NT, B_PASSES)
           for bi in range(len(starts))] for h in hs]
    yield
    sc = [[jnp.where(low[i0:i0 + B_SUB, 0:i0 + B_SUB], sc[h][bi], 0.0)
           for bi, i0 in enumerate(starts)] for h in hs]
    for h in hs:
        blocks = [_mm(sc[h][bi], v[0:i0 + B_SUB, vss[h]], NN, B_PASSES)
                  for bi, i0 in enumerate(starts)]
        o[h] = o[h] + jnp.concatenate(blocks, axis=0)
    for h in hs:
        s_ref[i, h] = st[h] * e_last[:, sls[h]] + _mm(v[:, vss[h]], k_end[:, sls[h]], TN, B_PASSES)
    yield
    for h in hs:
        vs = vss[h]
        o_h = o[h] * lax.rsqrt(jnp.mean(o[h] * o[h], axis=-1, keepdims=True) + EPS) * nw_ref[...]
        g_h = gate[:, vs]
        o_ref[i, :, vs] = (o_h * (g_h * jax.nn.sigmoid(g_h))).astype(o_ref.dtype)


def _gla_mixer(p, a2, ab, norm_w, out_dtype=BF16):
    bsz, lp, cols = p.shape
    rank, qk_width = a2.shape
    dv = norm_w.shape[-1]
    heads = qk_width // (dv // 2)
    v_width = heads * dv
    nb = _seqs_per_step(bsz, B_SEQS_PER_STEP)
    kern = functools.partial(_gla_kernel, qk_width=qk_width, v_width=v_width, rank=rank)
    full = lambda arr: pl.BlockSpec(arr.shape, lambda b, c: (0,) * arr.ndim)
    small = (a2, ab, norm_w)
    return pl.pallas_call(
        kern,
        grid=(bsz // nb, lp // CHUNK),
        in_specs=[pl.BlockSpec((nb, CHUNK, cols), lambda b, c: (b, c, 0))]
        + [full(t) for t in small],
        out_specs=pl.BlockSpec((nb, CHUNK, v_width), lambda b, c: (b, c, 0)),
        out_shape=jax.ShapeDtypeStruct((bsz, lp, v_width), out_dtype),
        scratch_shapes=[pltpu.VMEM((nb, heads, dv, qk_width // heads), F32)],
        compiler_params=_params(2),
    )(p, *small)


def _mlstm_kernel(p_ref, *rest, qk_width, v_width, heads):
    @pl.when(pl.program_id(1) == 0)
    def _():
        for ref in rest[-4:]:
            ref[...] = jnp.zeros_like(ref)

    _round_robin(_mlstm_chunk(i, p_ref, *rest, qk_width=qk_width, v_width=v_width,
                              heads=heads) for i in range(p_ref.shape[0]))


def _mlstm_chunk(i, p_ref, cw_ref, cb_ref, gb_ref, nw_ref, o_ref, cm_ref, n_ref, m_ref,
                 hist_ref, *, qk_width, v_width, heads):
    c = CHUNK
    dqk = qk_width // heads
    dv = v_width // heads

    off_v = 2 * qk_width
    off_g = off_v + v_width
    off_o = off_g + LANE
    p = p_ref[i]
    qk_raw = p[:, 0:off_v]
    v = p[:, off_v:off_g]
    gates = p[:, off_g:off_o]
    og = p[:, off_o:off_o + v_width]

    ext = jnp.concatenate([hist_ref[i], qk_raw], axis=0)
    hist_ref[i] = qk_raw[c - 8:c, :]
    conv = cb_ref[...] + ext * cw_ref[C_CONV - 1:C_CONV, :]
    for d in range(1, C_CONV):
        conv = conv + pltpu.roll(ext, d, axis=0) * cw_ref[C_CONV - 1 - d:C_CONV - d, :]
    conv = conv[8:, :]
    qk = conv * jax.nn.sigmoid(conv)
    q = qk[:, 0:qk_width]
    k = qk[:, qk_width:off_v] * (dqk ** -0.5)

    capped = C_GATE_CAP * jnp.tanh((gates + gb_ref[...]) / C_GATE_CAP)
    lf = _log_sigmoid(capped)
    yield
    bcol = _cumsum_rows(lf)
    brow = bcol.T
    irow = capped.T
    low = _tril_mask(c)
    hs = range(heads)
    qss = [slice(h * dqk, (h + 1) * dqk) for h in hs]
    vss = [slice(h * dv, (h + 1) * dv) for h in hs]
    np_ = 2 if C_PASSES == 3 else 1

    q_p = [_split(q[:, qss[h]], np_) for h in hs]
    k_p = [_split(k[:, qss[h]], np_) for h in hs]
    yield
    cm = [cm_ref[i, h] for h in hs]
    qk_s = [_mm_parts(q_p[h], k_p[h], NT) for h in hs]
    q_cm = [_mm_parts(q_p[h], _split(cm[h], np_), NN) for h in hs]

    yield
    bc = [bcol[:, heads + h:heads + h + 1] for h in hs]
    ic = [capped[:, h:h + 1] for h in hs]
    m_prev = [m_ref[i, h:h + 1, 0:1] for h in hs]
    n_row = [n_ref[i, h:h + 1, :] for h in hs]
    log_w = [jnp.where(low, bc[h] - brow[heads + h:heads + h + 1, :] + irow[h:h + 1, :],
                       -jnp.inf) for h in hs]
    log_prev = [bc[h] + m_prev[h] for h in hs]
    m_t = [jnp.maximum(log_prev[h], jnp.max(log_w[h], axis=-1, keepdims=True)) for h in hs]
    w_prev = [jnp.exp(log_prev[h] - m_t[h]) for h in hs]
    s = [qk_s[h] * jnp.exp(log_w[h] - m_t[h]) for h in hs]
    yield
    num = [w_prev[h] * q_cm[h] + _mm(s[h], v[:, vss[h]], NN, C_PASSES) for h in hs]
    den = [w_prev[h] * jnp.sum(q[:, qss[h]] * n_row[h], axis=-1, keepdims=True)
           + jnp.sum(s[h], axis=-1, keepdims=True) for h in hs]
    hh = [num[h] / jnp.maximum(jnp.abs(den[h]), jnp.exp(-m_t[h])) for h in hs]

    yield
    m_new = [m_t[h][c - 1:c, :] for h in hs]
    b_last = [bc[h][c - 1:c, :] for h in hs]
    kw = [k[:, qss[h]] * jnp.exp(b_last[h] - bc[h] + ic[h] - m_new[h]) for h in hs]
    f_end = [jnp.exp(b_last[h] + m_prev[h] - m_new[h]) for h in hs]
    for h in hs:
        cm_ref[i, h] = f_end[h] * cm[h] + _mm(kw[h], v[:, vss[h]], TN, C_PASSES)
        n_ref[i, h:h + 1, :] = f_end[h] * n_row[h] + jnp.sum(kw[h], axis=0, keepdims=True)
        m_ref[i, h:h + 1, :] = jnp.broadcast_to(m_new[h], (1, m_ref.shape[-1]))

    yield
    for h in hs:
        vs = vss[h]
        hn = hh[h] * lax.rsqrt(jnp.mean(hh[h] * hh[h], axis=-1, keepdims=True) + EPS)
        o_ref[i, :, vs] = (hn * nw_ref[:, vs] * jax.nn.sigmoid(og[:, vs])).astype(o_ref.dtype)


def _mlstm_mixer(p, conv_w, conv_b, gate_b, norm_w, heads, out_dtype=BF16):
    bsz, lp, cols = p.shape
    qk2 = conv_w.shape[-1]
    qk_width = qk2 // 2
    v_width = norm_w.shape[-1]
    nb = _seqs_per_step(bsz, C_SEQS_PER_STEP)
    kern = functools.partial(_mlstm_kernel, qk_width=qk_width, v_width=v_width, heads=heads)
    full = lambda arr: pl.BlockSpec(arr.shape, lambda b, c: (0,) * arr.ndim)
    small = (conv_w, conv_b, gate_b, norm_w)
    return pl.pallas_call(
        kern,
        grid=(bsz // nb, lp // CHUNK),
        in_specs=[pl.BlockSpec((nb, CHUNK, cols), lambda b, c: (b, c, 0))]
        + [full(t) for t in small],
        out_specs=pl.BlockSpec((nb, CHUNK, v_width), lambda b, c: (b, c, 0)),
        out_shape=jax.ShapeDtypeStruct((bsz, lp, v_width), out_dtype),
        scratch_shapes=[pltpu.VMEM((nb, heads, qk_width // heads, v_width // heads), F32),
                        pltpu.VMEM((nb, 8, qk_width // heads), F32),
                        pltpu.VMEM((nb, 8, LANE), F32),
                        pltpu.VMEM((nb, 8, qk2), F32)],
        compiler_params=_params(2),
    )(p, *small)


def _out_proj_kernel(h_ref, ya_ref, yb_ref, yc_ref, wa_ref, wb_ref, wc_ref, o_ref):
    acc = jnp.dot(ya_ref[...], wa_ref[...], preferred_element_type=F32)
    acc += jnp.dot(yb_ref[...], wb_ref[...], preferred_element_type=F32)
    acc += jnp.dot(yc_ref[...], wc_ref[...], preferred_element_type=F32)
    o_ref[...] = h_ref[...] + acc


def _out_proj(h, ya, yb, yc, wa, wb, wc, tm):
    m, d = h.shape
    row = lambda arr: pl.BlockSpec((tm, arr.shape[1]), lambda i: (i, 0))
    full = lambda arr: pl.BlockSpec(arr.shape, lambda i: (0, 0))
    return pl.pallas_call(
        _out_proj_kernel,
        grid=(m // tm,),
        in_specs=[row(h), row(ya), row(yb), row(yc), full(wa), full(wb), full(wc)],
        out_specs=pl.BlockSpec((tm, d), lambda i: (i, 0)),
        out_shape=jax.ShapeDtypeStruct((m, d), F32),
        compiler_params=_params(1),
    )(h, ya, yb, yc, wa, wb, wc)


def _ffn_kernel(h_ref, g_ref, w1_ref, w3_ref, w2_ref, gf_ref, o_ref, u_ref, *, final_norm,
                f_axis):
    f = pl.program_id(f_axis)

    @pl.when(f == 0)
    def _():
        x = h_ref[...]
        ms = jnp.mean(x * x, axis=-1, keepdims=True)
        u_ref[...] = (x * lax.rsqrt(ms + EPS) * g_ref[...]).astype(BF16)
        o_ref[...] = x

    u = u_ref[...]
    a = jnp.dot(u, w1_ref[...], preferred_element_type=F32)
    b = jnp.dot(u, w3_ref[...], preferred_element_type=F32)
    act = (a * jax.nn.sigmoid(a) * b).astype(BF16)
    o_ref[...] += jnp.dot(act, w2_ref[...], preferred_element_type=F32)

    if final_norm:
        @pl.when(f == pl.num_programs(f_axis) - 1)
        def _():
            y = o_ref[...]
            ms = jnp.mean(y * y, axis=-1, keepdims=True)
            o_ref[...] = y * lax.rsqrt(ms + EPS) * gf_ref[...]


def _ffn(h, g, w1, w3, w2, g_final, tm, tf, final_norm):
    m, d = h.shape
    dff = w1.shape[1]
    return pl.pallas_call(
        functools.partial(_ffn_kernel, final_norm=final_norm, f_axis=1),
        grid=(m // tm, dff // tf),
        in_specs=[
            pl.BlockSpec((tm, d), lambda i, f: (i, 0)),
            pl.BlockSpec((1, d), lambda i, f: (0, 0)),
            pl.BlockSpec((d, tf), lambda i, f: (0, f)),
            pl.BlockSpec((d, tf), lambda i, f: (0, f)),
            pl.BlockSpec((tf, d), lambda i, f: (f, 0)),
            pl.BlockSpec((1, d), lambda i, f: (0, 0)),
        ],
        out_specs=pl.BlockSpec((tm, d), lambda i, f: (i, 0)),
        out_shape=jax.ShapeDtypeStruct((m, d), F32),
        scratch_shapes=[pltpu.VMEM((tm, d), BF16)],
        compiler_params=_params(2),
    )(h, g, w1, w3, w2, g_final)


def _ffn_last(h3, g, w1, w3, w2, g_final, row0, rows, tm, tf):
    bsz, lp, d = h3.shape
    dff = w1.shape[1]
    return pl.pallas_call(
        functools.partial(_ffn_kernel, final_norm=True, f_axis=2),
        grid=(bsz, rows // tm, dff // tf),
        in_specs=[
            pl.BlockSpec((pl.Element(tm), pl.Element(d)),
                         lambda b, i, f: (pl.multiple_of(b * lp + row0 + i * tm, 8), 0)),
            pl.BlockSpec((1, d), lambda b, i, f: (0, 0)),
            pl.BlockSpec((d, tf), lambda b, i, f: (0, f)),
            pl.BlockSpec((d, tf), lambda b, i, f: (0, f)),
            pl.BlockSpec((tf, d), lambda b, i, f: (f, 0)),
            pl.BlockSpec((1, d), lambda b, i, f: (0, 0)),
        ],
        out_specs=pl.BlockSpec((None, tm, d), lambda b, i, f: (b, i, 0)),
        out_shape=jax.ShapeDtypeStruct((bsz, rows, d), F32),
        scratch_shapes=[pltpu.VMEM((tm, d), BF16)],
        compiler_params=_params(3),
    )(h3.reshape(bsz * lp, d), g, w1, w3, w2, g_final)


def _pad_groups(w, sizes, axis=-1):
    pieces = []
    off = 0
    for s in sizes:
        piece = lax.slice_in_dim(w, off, off + s, axis=axis)
        pad = _rup(s, LANE) - s
        if pad:
            cfg = [(0, 0)] * w.ndim
            cfg[axis] = (0, pad)
            piece = jnp.pad(piece, cfg)
        pieces.append(piece)
        off += s
    return jnp.concatenate(pieces, axis=axis)


def _tile_rows(m):
    for t in (640, 512, 320, 256, 128, 64, 32, 16, 8):
        if m % t == 0:
            return t
    return m


def _proj_tiles(m, d, n, in_bytes):
    budget = VMEM_LIMIT * 3 // 4
    col_tiles = [t for t in range(n, 0, -LANE) if n % t == 0]
    row_tiles = [t for t in (640, 320, 256, 128, 64, 32, 16, 8) if m % t == 0] or [m]
    for tn in col_tiles:
        for tm in row_tiles:
            need = 2 * tm * d * in_bytes + 2 * d * tn * 2 + 2 * tm * tn * 4
            if need <= budget:
                return tm, tn
    return row_tiles[-1], col_tiles[-1]


def _tile_cols(n, cap):
    best = LANE
    for t in range(LANE, cap + 1, LANE):
        if n % t == 0:
            best = t
    return best


def kernel(x, meta_tokens, norm_mix, w_in, rw_mu, rw_w0, rw_w2, rw_a0, rw_a2, rw_g2, rw_kk, rw_ka, rw_rk, rw_ln_w, rw_ln_b, gla_a2, gla_ab, gla_norm, ml_conv_w, ml_conv_b, ml_ib, ml_fb, ml_norm, w_out, norm_ffn, ffn_w1, ffn_w3, ffn_w2, norm_final):
    bsz, seq, d = x.shape
    depth = w_in.shape[0]
    n_meta = meta_tokens.shape[0]
    ltot = n_meta + seq
    lp = _rup(ltot, CHUNK)

    a_width = rw_w0.shape[-1]
    d_rank, a_rank, g_rank = rw_w2.shape[1], rw_a2.shape[1], rw_g2.shape[1]
    a_sizes = [a_width, a_width, a_width, d_rank, a_rank, g_rank]
    b_rank, b_qk = gla_a2.shape[1], gla_a2.shape[2]
    b_heads = b_qk // (gla_norm.shape[-1] // 2)
    b_width = b_heads * gla_norm.shape[-1]
    b_sizes = [b_qk, b_qk, b_width, b_rank, b_width]
    c_heads = ml_ib.shape[-1]
    c_qk = ml_conv_w.shape[-1] // 2
    c_width = ml_norm.shape[-1]
    c_sizes = [c_qk, c_qk, c_width, 2 * c_heads, c_width]
    a_cols, b_cols = sum(a_sizes), sum(b_sizes)

    meta = jnp.broadcast_to(meta_tokens[None].astype(x.dtype), (bsz, n_meta, d))
    h = jnp.concatenate([meta, x, jnp.zeros((bsz, lp - ltot, d), x.dtype)], axis=1)
    h = h.reshape(bsz * lp, d)
    m = bsz * lp
    tm = _tile_rows(m)
    row2 = lambda t: t.reshape(1, -1)

    w_out_bf = w_out.astype(BF16)
    w1_bf, w3_bf, w2_bf = ffn_w1.astype(BF16), ffn_w3.astype(BF16), ffn_w2.astype(BF16)
    for l in range(depth):
        w_l = w_in[l]
        wa = _pad_groups(w_l[:, :a_cols], a_sizes).astype(BF16)
        wb = _pad_groups(w_l[:, a_cols:a_cols + b_cols], b_sizes).astype(BF16)
        wc = _pad_groups(w_l[:, a_cols + b_cols:], c_sizes).astype(BF16)
        p_a, u = _norm_matmul(h, row2(norm_mix[l]), wa, *_proj_tiles(m, d, wa.shape[1], 6))
        p_b = _matmul(u, wb, *_proj_tiles(m, d, wb.shape[1], 2))
        p_c = _matmul(u, wc, *_proj_tiles(m, d, wc.shape[1], 2))

        y_a = _rwkv_mixer(
            p_a.reshape(bsz, lp, -1), _pad_groups(row2(rw_mu[l]), a_sizes),
            row2(rw_w0[l]), rw_w2[l], row2(rw_a0[l]), rw_a2[l], rw_g2[l], row2(rw_kk[l]),
            row2(rw_ka[l]), row2(rw_rk[l]), row2(rw_ln_w[l]), row2(rw_ln_b[l]))
        y_b = _gla_mixer(p_b.reshape(bsz, lp, -1), gla_a2[l], row2(gla_ab[l]),
                         row2(gla_norm[l]))
        gate_b = jnp.pad(jnp.concatenate([ml_ib[l], ml_fb[l]]), (0, LANE - 2 * c_heads))
        y_c = _mlstm_mixer(p_c.reshape(bsz, lp, -1), ml_conv_w[l], row2(ml_conv_b[l]),
                           row2(gate_b), row2(ml_norm[l]), c_heads)

        wo = w_out_bf[l]
        h = _out_proj(h, y_a.reshape(m, -1), y_b.reshape(m, -1), y_c.reshape(m, -1),
                      wo[:a_width], wo[a_width:a_width + b_width], wo[a_width + b_width:], tm)
        tf = _tile_cols(ffn_w1.shape[-1], 512)
        ffn_args = (row2(norm_ffn[l]), w1_bf[l], w3_bf[l], w2_bf[l], row2(norm_final))
        if l + 1 < depth:
            h = _ffn(h, *ffn_args, tm, tf, False)
        else:
            out = _ffn_last(h.reshape(bsz, lp, d), *ffn_args, n_meta, seq,
                            _tile_cols(seq, 1024), tf)
    return out
```

```python
import functools

import jax
import jax.numpy as jnp
from jax import lax
from jax.experimental import pallas as pl
from jax.experimental.pallas import tpu as pltpu

F32 = jnp.float32
BF16 = jnp.bfloat16

EPS = 1e-6
CHUNK = 64
A_SEQS_PER_STEP = 4
B_SEQS_PER_STEP = 4
C_SEQS_PER_STEP = 2

A_HEAD_DIM = 64
A_GN_EPS = 64e-5
B_DV = 128
B_GATE_NORMALIZER = 16.0
B_SUB = 16
C_CONV = 4
C_GATE_CAP = 15.0

LANE = 128
VMEM_LIMIT = 56 * 1024 * 1024

NN = (((1,), (0,)), ((), ()))
NT = (((1,), (1,)), ((), ()))
TN = (((0,), (0,)), ((), ()))

GATE_PASSES = 3
A_PASSES = 1
A_INV_PASSES = 1
B_PASSES = 1
C_PASSES = 1


def _rup(n, m):
    return -(-n // m) * m


def _split(x, parts):
    out = []
    rem = x
    for i in range(parts):
        hi = rem.astype(BF16)
        out.append(hi)
        if i + 1 < parts:
            rem = rem - hi.astype(F32)
    return out


def _mm_parts(a_parts, b_parts, dims):
    dg = lambda a, b: lax.dot_general(a, b, dims, preferred_element_type=F32)
    acc = dg(a_parts[0], b_parts[0])
    if len(a_parts) > 1:
        acc = acc + dg(a_parts[0], b_parts[1]) + dg(a_parts[1], b_parts[0])
    return acc


def _mm(a, b, dims=NN, passes=1):
    n = 2 if passes == 3 else 1
    return _mm_parts(_split(a, n), _split(b, n), dims)


def _cumsum_rows(x):
    tri = _tril_mask(x.shape[0]).astype(BF16)
    hi, mid, lo = (jnp.dot(tri, part, preferred_element_type=F32) for part in _split(x, 3))
    return hi + (mid + lo)


def _softplus(z):
    return jnp.maximum(z, 0.0) + jnp.log1p(jnp.exp(-jnp.abs(z)))


def _log_sigmoid(z):
    return -_softplus(-z)


def _tril_mask(n, strict=False):
    r = lax.broadcasted_iota(jnp.int32, (n, n), 0)
    c = lax.broadcasted_iota(jnp.int32, (n, n), 1)
    return (r > c) if strict else (r >= c)


def _params(n_axes):
    return pltpu.CompilerParams(
        dimension_semantics=("arbitrary",) * n_axes, vmem_limit_bytes=VMEM_LIMIT)


def _norm_matmul_kernel(h_ref, g_ref, w_ref, o_ref, u_ref):
    @pl.when(pl.program_id(1) == 0)
    def _():
        x = h_ref[...]
        ms = jnp.mean(x * x, axis=-1, keepdims=True)
        u_ref[...] = (x * lax.rsqrt(ms + EPS) * g_ref[...]).astype(BF16)

    o_ref[...] = jnp.dot(u_ref[...], w_ref[...], preferred_element_type=F32)


def _norm_matmul(h, g, w, tm, tn):
    m, d = h.shape
    n = w.shape[1]
    return pl.pallas_call(
        _norm_matmul_kernel,
        grid=(m // tm, n // tn),
        in_specs=[
            pl.BlockSpec((tm, d), lambda i, j: (i, 0)),
            pl.BlockSpec((1, d), lambda i, j: (0, 0)),
            pl.BlockSpec((d, tn), lambda i, j: (0, j)),
        ],
        out_specs=[pl.BlockSpec((tm, tn), lambda i, j: (i, j)),
                   pl.BlockSpec((tm, d), lambda i, j: (i, 0))],
        out_shape=[jax.ShapeDtypeStruct((m, n), F32), jax.ShapeDtypeStruct((m, d), BF16)],
        compiler_params=_params(2),
    )(h, g, w)


def _matmul_kernel(u_ref, w_ref, o_ref):
    o_ref[...] = jnp.dot(u_ref[...], w_ref[...], preferred_element_type=F32)


def _matmul(u, w, tm, tn):
    m, d = u.shape
    n = w.shape[1]
    return pl.pallas_call(
        _matmul_kernel,
        grid=(m // tm, n // tn),
        in_specs=[pl.BlockSpec((tm, d), lambda i, j: (i, 0)),
                  pl.BlockSpec((d, tn), lambda i, j: (0, j))],
        out_specs=pl.BlockSpec((tm, tn), lambda i, j: (i, j)),
        out_shape=jax.ShapeDtypeStruct((m, n), F32),
        compiler_params=_params(2),
    )(u, w)


def _seqs_per_step(bsz, want):
    return want if bsz % want == 0 else 1


def _round_robin(gens):
    gens = list(gens)
    while gens:
        alive = []
        for gen in gens:
            try:
                next(gen)
                alive.append(gen)
            except StopIteration:
                pass
        gens = alive


def _rwkv_kernel(p_ref, *rest, width, d_rank, a_rank, g_rank):
    s_ref, prev_ref = rest[-2:]

    @pl.when(pl.program_id(1) == 0)
    def _():
        s_ref[...] = jnp.zeros_like(s_ref)
        prev_ref[...] = jnp.zeros_like(prev_ref)

    _round_robin(_rwkv_chunk(i, p_ref, *rest, width=width, d_rank=d_rank, a_rank=a_rank,
                             g_rank=g_rank) for i in range(p_ref.shape[0]))


def _rwkv_chunk(i, p_ref, mu_ref, w0_ref, w2_ref, a0_ref, a2_ref, g2_ref, kk_ref, ka_ref,
                rk_ref, lnw_ref, lnb_ref, o_ref, s_ref, prev_ref, *, width, d_rank,
                a_rank, g_rank):
    c = CHUNK
    n = A_HEAD_DIM
    heads = width // n

    p = p_ref[i]
    row = lax.broadcasted_iota(jnp.int32, p.shape, 0)
    shifted = jnp.where(row == 0, prev_ref[i], pltpu.roll(p, 1, axis=0))
    prev_ref[i] = p[c - 1:c, :]
    x = p + mu_ref[...] * (shifted - p)

    off_w = 3 * width
    off_a = off_w + _rup(d_rank, LANE)
    off_g = off_a + _rup(a_rank, LANE)
    r = x[:, 0:width]
    k = x[:, width:2 * width]
    v = x[:, 2 * width:3 * width]
    xw = x[:, off_w:off_w + d_rank]
    xa = x[:, off_a:off_a + a_rank]
    xg = x[:, off_g:off_g + g_rank]

    w_log = -_softplus(-(w0_ref[...] + _mm(jnp.tanh(xw), w2_ref[...], NN, GATE_PASSES))) - 0.5
    ld = -jnp.exp(w_log)
    a = jax.nn.sigmoid(a0_ref[...] + _mm(xa, a2_ref[...], NN, GATE_PASSES))
    g = _mm(jax.nn.sigmoid(xg), g2_ref[...], NN, GATE_PASSES)
    kk = k * kk_ref[...]
    k2 = k * (1.0 + (a - 1.0) * ka_ref[...])

    yield
    cum = _cumsum_rows(ld)
    e_pos = jnp.exp(cum)
    e_neg = jnp.exp(-cum)
    e_prev = jnp.exp(cum - ld)
    e_end = jnp.exp(cum[c - 1:c, :] - cum)

    assert c == n and 2 * n == LANE
    pairs = heads // 2
    ps = [slice(j * LANE, (j + 1) * LANE) for j in range(pairs)]
    js = range(pairs)
    first = lax.broadcasted_iota(jnp.int32, (1, LANE), 1) < n
    rows = lax.broadcasted_iota(jnp.int32, (c, 4 * c), 0)
    cols = lax.broadcasted_iota(jnp.int32, (c, 4 * c), 1) % c
    low4 = rows >= cols
    slow2 = (rows > cols)[:, 0:LANE]
    eye2 = (rows == cols)[:, 0:LANE].astype(F32)

    def bdiag(x):
        return jnp.concatenate([jnp.where(first, x, 0.0), jnp.where(first, 0.0, x)], axis=0)

    same_head = (lax.broadcasted_iota(jnp.int32, (LANE, LANE), 0) // n
                 == lax.broadcasted_iota(jnp.int32, (LANE, LANE), 1) // n).astype(BF16)

    def head_sum(x, parts):
        acc = None
        for part in reversed(_split(x, parts)):
            t = jnp.concatenate([jnp.dot(part[:, sl], same_head, preferred_element_type=F32)
                                 for sl in ps], axis=1)
            acc = t if acc is None else acc + t
        return acc

    yield
    kk = kk / jnp.maximum(jnp.sqrt(head_sum(kk * kk, 2)), 1e-12)
    b = kk * a
    lhs_f = jnp.concatenate([-kk * e_prev, r * e_pos], axis=0)
    rhs_f = jnp.concatenate([b * e_neg, k2 * e_neg], axis=0)
    end_f = jnp.concatenate([b * e_end, k2 * e_end], axis=0)

    yield
    s0 = [s_ref[i, j] for j in js]
    big = [_mm(lhs_f[:, ps[j]],
               jnp.concatenate([bdiag(rhs_f[0:c, ps[j]]), bdiag(rhs_f[c:2 * c, ps[j]]),
                                bdiag(s0[j])], axis=0), NT, A_PASSES) for j in js]

    yield
    a_ab = [jnp.where(slow2, big[j][0:c, 0:LANE], 0.0) for j in js]
    v_bd = [bdiag(v[:, ps[j]]) for j in js]
    rhs_u = [big[j][0:c, 2 * LANE:3 * LANE]
             + _mm(jnp.where(slow2, big[j][0:c, LANE:2 * LANE], 0.0), v_bd[j], NN, A_PASSES)
             for j in js]

    yield
    tinv = [eye2 + a_ab[j] for j in js]
    pw = [_mm(a_ab[j], bdiag(a_ab[j]), NN, A_INV_PASSES) for j in js]
    for _ in range((c - 1).bit_length() - 2):
        yield
        res = [_mm(jnp.concatenate([pw[j], tinv[j]], axis=0), bdiag(pw[j]), NN, A_INV_PASSES)
               for j in js]
        pw = [res[j][0:c] for j in js]
        tinv = [tinv[j] + res[j][c:2 * c] for j in js]
    yield
    tinv = [tinv[j] + _mm(tinv[j], bdiag(pw[j]), NN, A_INV_PASSES) for j in js]
    yield
    u = [_mm(tinv[j], bdiag(rhs_u[j]), NN, A_INV_PASSES) for j in js]
    yield
    y = [big[j][c:2 * c, 2 * LANE:3 * LANE]
         + _mm(jnp.where(low4, big[j][c:2 * c, 0:2 * LANE], 0.0),
               jnp.concatenate([bdiag(u[j]), v_bd[j]], axis=0), NN, A_PASSES) for j in js]
    for j in js:
        uv = jnp.concatenate([u[j], v[:, ps[j]]], axis=0)
        full = _mm(uv, end_f[:, ps[j]], TN, A_PASSES)
        s_ref[i, j] = (s0[j] * e_pos[c - 1:c, ps[j]]
                       + jnp.where(first, full[0:n], full[n:2 * n]))

    yield
    y = jnp.concatenate(y, axis=1)
    sums = head_sum(jnp.concatenate([y, r * k2 * rk_ref[...]], axis=0), 2)
    yc = y - sums[0:c] * (1.0 / n)
    var = head_sum(yc * yc, 2) * (1.0 / n)
    yn = yc * lax.rsqrt(var + A_GN_EPS) * lnw_ref[...] + lnb_ref[...]
    bonus = sums[c:2 * c] * v
    o_ref[i] = ((yn + bonus) * g).astype(o_ref.dtype)


def _rwkv_mixer(p, mu, w0, w2, a0, a2, g2, k_k, k_a, r_k, ln_w, ln_b, out_dtype=BF16):
    bsz, lp, cols = p.shape
    nb = _seqs_per_step(bsz, A_SEQS_PER_STEP)
    width = w0.shape[-1]
    heads = width // A_HEAD_DIM
    kern = functools.partial(_rwkv_kernel, width=width, d_rank=w2.shape[0],
                             a_rank=a2.shape[0], g_rank=g2.shape[0])
    full = lambda arr: pl.BlockSpec(arr.shape, lambda b, c: (0,) * arr.ndim)
    small = (mu, w0, w2, a0, a2, g2, k_k, k_a, r_k, ln_w, ln_b)
    return pl.pallas_call(
        kern,
        grid=(bsz // nb, lp // CHUNK),
        in_specs=[pl.BlockSpec((nb, CHUNK, cols), lambda b, c: (b, c, 0))]
        + [full(t) for t in small],
        out_specs=pl.BlockSpec((nb, CHUNK, width), lambda b, c: (b, c, 0)),
        out_shape=jax.ShapeDtypeStruct((bsz, lp, width), out_dtype),
        scratch_shapes=[pltpu.VMEM((nb, heads // 2, A_HEAD_DIM, 2 * A_HEAD_DIM), F32),
                        pltpu.VMEM((nb, 1, cols), F32)],
        compiler_params=_params(2),
    )(p, *small)


def _gla_kernel(p_ref, *rest, qk_width, v_width, rank):
    s_ref = rest[-1]

    @pl.when(pl.program_id(1) == 0)
    def _():
        s_ref[...] = jnp.zeros_like(s_ref)

    _round_robin(_gla_chunk(i, p_ref, *rest, qk_width=qk_width, v_width=v_width, rank=rank)
                 for i in range(p_ref.shape[0]))


def _gla_chunk(i, p_ref, a2_ref, ab_ref, nw_ref, o_ref, s_ref, *, qk_width, v_width, rank):
    c = CHUNK
    dv = B_DV
    heads = v_width // dv
    dk = qk_width // heads

    off_v = 2 * qk_width
    off_a = off_v + v_width
    off_g = off_a + _rup(rank, LANE)
    p = p_ref[i]
    q = p[:, 0:qk_width] * (dk ** -0.5)
    k = p[:, qk_width:off_v]
    v = p[:, off_v:off_a]
    xa = p[:, off_a:off_a + rank]
    gate = p[:, off_g:off_g + v_width]

    lg = _log_sigmoid(_mm(xa, a2_ref[...], NN, GATE_PASSES) + ab_ref[...]) / B_GATE_NORMALIZER
    yield
    bcum = _cumsum_rows(lg)
    low = _tril_mask(c)
    b_last = bcum[c - 1:c, :]
    q_in = q * jnp.exp(bcum)
    k_end = k * jnp.exp(b_last - bcum)
    e_last = jnp.exp(b_last)

    qf, kf = [], []
    for i0 in range(0, c, B_SUB):
        i1 = i0 + B_SUB
        ref_pt = bcum[i1 - 1:i1, :]
        qf.append(q[i0:i1] * jnp.exp(bcum[i0:i1] - ref_pt))
        kf.append(k[0:i1] * jnp.exp(ref_pt - bcum[0:i1]))

    hs = range(heads)
    sls = [slice(h * dk, (h + 1) * dk) for h in hs]
    vss = [slice(h * dv, (h + 1) * dv) for h in hs]
    yield
    st = [s_ref[i, h] for h in hs]
    o = [_mm(q_in[:, sls[h]], st[h], NT, B_PASSES) for h in hs]
    starts = list(range(0, c, B_SUB))
    sc = [[_mm(qf[bi][:, sls[h]], kf[bi][:, sls[h]], NT, B_PASSES)
           for bi in range(len(starts))] for h in hs]
    yield
    sc = [[jnp.where(low[i0:i0 + B_SUB, 0:i0 + B_SUB], sc[h][bi], 0.0)
           for bi, i0 in enumerate(starts)] for h in hs]
    for h in hs:
        blocks = [_mm(sc[h][bi], v[0:i0 + B_SUB, vss[h]], NN, B_PASSES)
                  for bi, i0 in enumerate(starts)]
        o[h] = o[h] + jnp.concatenate(blocks, axis=0)
    for h in hs:
        s_ref[i, h] = st[h] * e_last[:, sls[h]] + _mm(v[:, vss[h]], k_end[:, sls[h]], TN, B_PASSES)
    yield
    for h in hs:
        vs = vss[h]
        o_h = o[h] * lax.rsqrt(jnp.mean(o[h] * o[h], axis=-1, keepdims=True) + EPS) * nw_ref[...]
        g_h = gate[:, vs]
        o_ref[i, :, vs] = (o_h * (g_h * jax.nn.sigmoid(g_h))).astype(o_ref.dtype)


def _gla_mixer(p, a2, ab, norm_w, out_dtype=BF16):
    bsz, lp, cols = p.shape
    rank, qk_width = a2.shape
    dv = norm_w.shape[-1]
    heads = qk_width // (dv // 2)
    v_width = heads * dv
    nb = _seqs_per_step(bsz, B_SEQS_PER_STEP)
    kern = functools.partial(_gla_kernel, qk_width=qk_width, v_width=v_width, rank=rank)
    full = lambda arr: pl.BlockSpec(arr.shape, lambda b, c: (0,) * arr.ndim)
    small = (a2, ab, norm_w)
    return pl.pallas_call(
        kern,
        grid=(bsz // nb, lp // CHUNK),
        in_specs=[pl.BlockSpec((nb, CHUNK, cols), lambda b, c: (b, c, 0))]
        + [full(t) for t in small],
        out_specs=pl.BlockSpec((nb, CHUNK, v_width), lambda b, c: (b, c, 0)),
        out_shape=jax.ShapeDtypeStruct((bsz, lp, v_width), out_dtype),
        scratch_shapes=[pltpu.VMEM((nb, heads, dv, qk_width // heads), F32)],
        compiler_params=_params(2),
    )(p, *small)


def _mlstm_kernel(p_ref, *rest, qk_width, v_width, heads):
    @pl.when(pl.program_id(1) == 0)
    def _():
        for ref in rest[-4:]:
            ref[...] = jnp.zeros_like(ref)

    _round_robin(_mlstm_chunk(i, p_ref, *rest, qk_width=qk_width, v_width=v_width,
                              heads=heads) for i in range(p_ref.shape[0]))


def _mlstm_chunk(i, p_ref, cw_ref, cb_ref, gb_ref, nw_ref, o_ref, cm_ref, n_ref, m_ref,
                 hist_ref, *, qk_width, v_width, heads):
    c = CHUNK
    dqk = qk_width // heads
    dv = v_width // heads

    off_v = 2 * qk_width
    off_g = off_v + v_width
    off_o = off_g + LANE
    p = p_ref[i]
    qk_raw = p[:, 0:off_v]
    v = p[:, off_v:off_g]
    gates = p[:, off_g:off_o]
    og = p[:, off_o:off_o + v_width]

    ext = jnp.concatenate([hist_ref[i], qk_raw], axis=0)
    hist_ref[i] = qk_raw[c - 8:c, :]
    conv = cb_ref[...] + ext * cw_ref[C_CONV - 1:C_CONV, :]
    for d in range(1, C_CONV):
        conv = conv + pltpu.roll(ext, d, axis=0) * cw_ref[C_CONV - 1 - d:C_CONV - d, :]
    conv = conv[8:, :]
    qk = conv * jax.nn.sigmoid(conv)
    q = qk[:, 0:qk_width]
    k = qk[:, qk_width:off_v] * (dqk ** -0.5)

    capped = C_GATE_CAP * jnp.tanh((gates + gb_ref[...]) / C_GATE_CAP)
    lf = _log_sigmoid(capped)
    yield
    bcol = _cumsum_rows(lf)
    brow = bcol.T
    irow = capped.T
    low = _tril_mask(c)
    hs = range(heads)
    qss = [slice(h * dqk, (h + 1) * dqk) for h in hs]
    vss = [slice(h * dv, (h + 1) * dv) for h in hs]
    np_ = 2 if C_PASSES == 3 else 1

    q_p = [_split(q[:, qss[h]], np_) for h in hs]
    k_p = [_split(k[:, qss[h]], np_) for h in hs]
    yield
    cm = [cm_ref[i, h] for h in hs]
    qk_s = [_mm_parts(q_p[h], k_p[h], NT) for h in hs]
    q_cm = [_mm_parts(q_p[h], _split(cm[h], np_), NN) for h in hs]

    yield
    bc = [bcol[:, heads + h:heads + h + 1] for h in hs]
    ic = [capped[:, h:h + 1] for h in hs]
    m_prev = [m_ref[i, h:h + 1, 0:1] for h in hs]
    n_row = [n_ref[i, h:h + 1, :] for h in hs]
    log_w = [jnp.where(low, bc[h] - brow[heads + h:heads + h + 1, :] + irow[h:h + 1, :],
                       -jnp.inf) for h in hs]
    log_prev = [bc[h] + m_prev[h] for h in hs]
    m_t = [jnp.maximum(log_prev[h], jnp.max(log_w[h], axis=-1, keepdims=True)) for h in hs]
    w_prev = [jnp.exp(log_prev[h] - m_t[h]) for h in hs]
    s = [qk_s[h] * jnp.exp(log_w[h] - m_t[h]) for h in hs]
    yield
    num = [w_prev[h] * q_cm[h] + _mm(s[h], v[:, vss[h]], NN, C_PASSES) for h in hs]
    den = [w_prev[h] * jnp.sum(q[:, qss[h]] * n_row[h], axis=-1, keepdims=True)
           + jnp.sum(s[h], axis=-1, keepdims=True) for h in hs]
    hh = [num[h] / jnp.maximum(jnp.abs(den[h]), jnp.exp(-m_t[h])) for h in hs]

    yield
    m_new = [m_t[h][c - 1:c, :] for h in hs]
    b_last = [bc[h][c - 1:c, :] for h in hs]
    kw = [k[:, qss[h]] * jnp.exp(b_last[h] - bc[h] + ic[h] - m_new[h]) for h in hs]
    f_end = [jnp.exp(b_last[h] + m_prev[h] - m_new[h]) for h in hs]
    for h in hs:
        cm_ref[i, h] = f_end[h] * cm[h] + _mm(kw[h], v[:, vss[h]], TN, C_PASSES)
        n_ref[i, h:h + 1, :] = f_end[h] * n_row[h] + jnp.sum(kw[h], axis=0, keepdims=True)
        m_ref[i, h:h + 1, :] = jnp.broadcast_to(m_new[h], (1, m_ref.shape[-1]))

    yield
    for h in hs:
        vs = vss[h]
        hn = hh[h] * lax.rsqrt(jnp.mean(hh[h] * hh[h], axis=-1, keepdims=True) + EPS)
        o_ref[i, :, vs] = (hn * nw_ref[:, vs] * jax.nn.sigmoid(og[:, vs])).astype(o_ref.dtype)


def _mlstm_mixer(p, conv_w, conv_b, gate_b, norm_w, heads, out_dtype=BF16):
    bsz, lp, cols = p.shape
    qk2 = conv_w.shape[-1]
    qk_width = qk2 // 2
    v_width = norm_w.shape[-1]
    nb = _seqs_per_step(bsz, C_SEQS_PER_STEP)
    kern = functools.partial(_mlstm_kernel, qk_width=qk_width, v_width=v_width, heads=heads)
    full = lambda arr: pl.BlockSpec(arr.shape, lambda b, c: (0,) * arr.ndim)
    small = (conv_w, conv_b, gate_b, norm_w)
    return pl.pallas_call(
        kern,
        grid=(bsz // nb, lp // CHUNK),
        in_specs=[pl.BlockSpec((nb, CHUNK, cols), lambda b, c: (b, c, 0))]
        + [full(t) for t in small],
        out_specs=pl.BlockSpec((nb, CHUNK, v_width), lambda b, c: (b, c, 0)),
        out_shape=jax.ShapeDtypeStruct((bsz, lp, v_width), out_dtype),
        scratch_shapes=[pltpu.VMEM((nb, heads, qk_width // heads, v_width // heads), F32),
                        pltpu.VMEM((nb, 8, qk_width // heads), F32),
                        pltpu.VMEM((nb, 8, LANE), F32),
                        pltpu.VMEM((nb, 8, qk2), F32)],
        compiler_params=_params(2),
    )(p, *small)


def _out_proj_kernel(h_ref, ya_ref, yb_ref, yc_ref, wa_ref, wb_ref, wc_ref, o_ref):
    acc = jnp.dot(ya_ref[...], wa_ref[...], preferred_element_type=F32)
    acc += jnp.dot(yb_ref[...], wb_ref[...], preferred_element_type=F32)
    acc += jnp.dot(yc_ref[...], wc_ref[...], preferred_element_type=F32)
    o_ref[...] = h_ref[...] + acc


def _out_proj(h, ya, yb, yc, wa, wb, wc, tm):
    m, d = h.shape
    row = lambda arr: pl.BlockSpec((tm, arr.shape[1]), lambda i: (i, 0))
    full = lambda arr: pl.BlockSpec(arr.shape, lambda i: (0, 0))
    return pl.pallas_call(
        _out_proj_kernel,
        grid=(m // tm,),
        in_specs=[row(h), row(ya), row(yb), row(yc), full(wa), full(wb), full(wc)],
        out_specs=pl.BlockSpec((tm, d), lambda i: (i, 0)),
        out_shape=jax.ShapeDtypeStruct((m, d), F32),
        compiler_params=_params(1),
    )(h, ya, yb, yc, wa, wb, wc)


def _ffn_kernel(h_ref, g_ref, w1_ref, w3_ref, w2_ref, gf_ref, o_ref, u_ref, *, final_norm,
                f_axis):
    f = pl.program_id(f_axis)

    @pl.when(f == 0)
    def _():
        x = h_ref[...]
        ms = jnp.mean(x * x, axis=-1, keepdims=True)
        u_ref[...] = (x * lax.rsqrt(ms + EPS) * g_ref[...]).astype(BF16)
        o_ref[...] = x

    u = u_ref[...]
    a = jnp.dot(u, w1_ref[...], preferred_element_type=F32)
    b = jnp.dot(u, w3_ref[...], preferred_element_type=F32)
    act = (a * jax.nn.sigmoid(a) * b).astype(BF16)
    o_ref[...] += jnp.dot(act, w2_ref[...], preferred_element_type=F32)

    if final_norm:
        @pl.when(f == pl.num_programs(f_axis) - 1)
        def _():
            y = o_ref[...]
            ms = jnp.mean(y * y, axis=-1, keepdims=True)
            o_ref[...] = y * lax.rsqrt(ms + EPS) * gf_ref[...]


def _ffn(h, g, w1, w3, w2, g_final, tm, tf, final_norm):
    m, d = h.shape
    dff = w1.shape[1]
    return pl.pallas_call(
        functools.partial(_ffn_kernel, final_norm=final_norm, f_axis=1),
        grid=(m // tm, dff // tf),
        in_specs=[
            pl.BlockSpec((tm, d), lambda i, f: (i, 0)),
            pl.BlockSpec((1, d), lambda i, f: (0, 0)),
            pl.BlockSpec((d, tf), lambda i, f: (0, f)),
            pl.BlockSpec((d, tf), lambda i, f: (0, f)),
            pl.BlockSpec((tf, d), lambda i, f: (f, 0)),
            pl.BlockSpec((1, d), lambda i, f: (0, 0)),
        ],
        out_specs=pl.BlockSpec((tm, d), lambda i, f: (i, 0)),
        out_shape=jax.ShapeDtypeStruct((m, d), F32),
        scratch_shapes=[pltpu.VMEM((tm, d), BF16)],
        compiler_params=_params(2),
    )(h, g, w1, w3, w2, g_final)


def _ffn_last(h3, g, w1, w3, w2, g_final, row0, rows, tm, tf):
    bsz, lp, d = h3.shape
    dff = w1.shape[1]
    return pl.pallas_call(
        functools.partial(_ffn_kernel, final_norm=True, f_axis=2),
        grid=(bsz, rows // tm, dff // tf),
        in_specs=[
            pl.BlockSpec((pl.Element(tm), pl.Element(d)),
                         lambda b, i, f: (pl.multiple_of(b * lp + row0 + i * tm, 8), 0)),
            pl.BlockSpec((1, d), lambda b, i, f: (0, 0)),
            pl.BlockSpec((d, tf), lambda b, i, f: (0, f)),
            pl.BlockSpec((d, tf), lambda b, i, f: (0, f)),
            pl.BlockSpec((tf, d), lambda b, i, f: (f, 0)),
            pl.BlockSpec((1, d), lambda b, i, f: (0, 0)),
        ],
        out_specs=pl.BlockSpec((None, tm, d), lambda b, i, f: (b, i, 0)),
        out_shape=jax.ShapeDtypeStruct((bsz, rows, d), F32),
        scratch_shapes=[pltpu.VMEM((tm, d), BF16)],
        compiler_params=_params(3),
    )(h3.reshape(bsz * lp, d), g, w1, w3, w2, g_final)


def _pad_groups(w, sizes, axis=-1):
    pieces = []
    off = 0
    for s in sizes:
        piece = lax.slice_in_dim(w, off, off + s, axis=axis)
        pad = _rup(s, LANE) - s
        if pad:
            cfg = [(0, 0)] * w.ndim
            cfg[axis] = (0, pad)
            piece = jnp.pad(piece, cfg)
        pieces.append(piece)
        off += s
    return jnp.concatenate(pieces, axis=axis)


def _tile_rows(m):
    for t in (832, 640, 512, 320, 256, 128, 64, 32, 16, 8):
        if m % t == 0:
            return t
    return m


def _proj_tiles(m, d, n, in_bytes):
    budget = VMEM_LIMIT * 3 // 4
    col_tiles = [t for t in range(n, 0, -LANE) if n % t == 0]
    row_tiles = [t for t in (640, 320, 256, 128, 64, 32, 16, 8) if m % t == 0] or [m]
    for tn in col_tiles:
        for tm in row_tiles:
            need = 2 * tm * d * in_bytes + 2 * d * tn * 2 + 2 * tm * tn * 4
            if need <= budget:
                return tm, tn
    return row_tiles[-1], col_tiles[-1]


def _tile_cols(n, cap):
    best = LANE
    for t in range(LANE, cap + 1, LANE):
        if n % t == 0:
            best = t
    return best


def kernel(x, meta_tokens, norm_mix, w_in, rw_mu, rw_w0, rw_w2, rw_a0, rw_a2, rw_g2, rw_kk, rw_ka, rw_rk, rw_ln_w, rw_ln_b, gla_a2, gla_ab, gla_norm, ml_conv_w, ml_conv_b, ml_ib, ml_fb, ml_norm, w_out, norm_ffn, ffn_w1, ffn_w3, ffn_w2, norm_final):
    bsz, seq, d = x.shape
    depth = w_in.shape[0]
    n_meta = meta_tokens.shape[0]
    ltot = n_meta + seq
    lp = _rup(ltot, CHUNK)

    a_width = rw_w0.shape[-1]
    d_rank, a_rank, g_rank = rw_w2.shape[1], rw_a2.shape[1], rw_g2.shape[1]
    a_sizes = [a_width, a_width, a_width, d_rank, a_rank, g_rank]
    b_rank, b_qk = gla_a2.shape[1], gla_a2.shape[2]
    b_heads = b_qk // (gla_norm.shape[-1] // 2)
    b_width = b_heads * gla_norm.shape[-1]
    b_sizes = [b_qk, b_qk, b_width, b_rank, b_width]
    c_heads = ml_ib.shape[-1]
    c_qk = ml_conv_w.shape[-1] // 2
    c_width = ml_norm.shape[-1]
    c_sizes = [c_qk, c_qk, c_width, 2 * c_heads, c_width]
    a_cols, b_cols = sum(a_sizes), sum(b_sizes)

    meta = jnp.broadcast_to(meta_tokens[None].astype(x.dtype), (bsz, n_meta, d))
    h = jnp.concatenate([meta, x, jnp.zeros((bsz, lp - ltot, d), x.dtype)], axis=1)
    h = h.reshape(bsz * lp, d)
    m = bsz * lp
    tm = _tile_rows(m)
    row2 = lambda t: t.reshape(1, -1)

    w_out_bf = w_out.astype(BF16)
    w1_bf, w3_bf, w2_bf = ffn_w1.astype(BF16), ffn_w3.astype(BF16), ffn_w2.astype(BF16)
    for l in range(depth):
        w_l = w_in[l]
        wa = _pad_groups(w_l[:, :a_cols], a_sizes).astype(BF16)
        wb = _pad_groups(w_l[:, a_cols:a_cols + b_cols], b_sizes).astype(BF16)
        wc = _pad_groups(w_l[:, a_cols + b_cols:], c_sizes).astype(BF16)
        p_a, u = _norm_matmul(h, row2(norm_mix[l]), wa, *_proj_tiles(m, d, wa.shape[1], 6))
        p_b = _matmul(u, wb, *_proj_tiles(m, d, wb.shape[1], 2))
        p_c = _matmul(u, wc, *_proj_tiles(m, d, wc.shape[1], 2))

        y_a = _rwkv_mixer(
            p_a.reshape(bsz, lp, -1), _pad_groups(row2(rw_mu[l]), a_sizes),
            row2(rw_w0[l]), rw_w2[l], row2(rw_a0[l]), rw_a2[l], rw_g2[l], row2(rw_kk[l]),
            row2(rw_ka[l]), row2(rw_rk[l]), row2(rw_ln_w[l]), row2(rw_ln_b[l]))
        y_b = _gla_mixer(p_b.reshape(bsz, lp, -1), gla_a2[l], row2(gla_ab[l]),
                         row2(gla_norm[l]))
        gate_b = jnp.pad(jnp.concatenate([ml_ib[l], ml_fb[l]]), (0, LANE - 2 * c_heads))
        y_c = _mlstm_mixer(p_c.reshape(bsz, lp, -1), ml_conv_w[l], row2(ml_conv_b[l]),
                           row2(gate_b), row2(ml_norm[l]), c_heads)

        wo = w_out_bf[l]
        h = _out_proj(h, y_a.reshape(m, -1), y_b.reshape(m, -1), y_c.reshape(m, -1),
                      wo[:a_width], wo[a_width:a_width + b_width], wo[a_width + b_width:], tm)
        tf = _tile_cols(ffn_w1.shape[-1], 512)
        ffn_args = (row2(norm_ffn[l]), w1_bf[l], w3_bf[l], w2_bf[l], row2(norm_final))
        if l + 1 < depth:
            h = _ffn(h, *ffn_args, tm, tf, False)
        else:
            out = _ffn_last(h.reshape(bsz, lp, d), *ffn_args, n_meta, seq,
                            _tile_cols(seq, 1024), tf)
    return out
```

```python
import functools

import jax
import jax.numpy as jnp
from jax import lax
from jax.experimental import pallas as pl
from jax.experimental.pallas import tpu as pltpu

F32 = jnp.float32
BF16 = jnp.bfloat16

EPS = 1e-6
CHUNK = 64
A_SEQS_PER_STEP = 4
B_SEQS_PER_STEP = 4
C_SEQS_PER_STEP = 4

A_HEAD_DIM = 64
A_GN_EPS = 64e-5
B_DV = 128
B_GATE_NORMALIZER = 16.0
B_SUB = 16
C_CONV = 4
C_GATE_CAP = 15.0

LANE = 128
VMEM_LIMIT = 56 * 1024 * 1024

NN = (((1,), (0,)), ((), ()))
NT = (((1,), (1,)), ((), ()))
TN = (((0,), (0,)), ((), ()))

GATE_PASSES = 3
A_PASSES = 1
A_INV_PASSES = 1
B_PASSES = 1
C_PASSES = 1


def _rup(n, m):
    return -(-n // m) * m


def _split(x, parts):
    out = []
    rem = x
    for i in range(parts):
        hi = rem.astype(BF16)
        out.append(hi)
        if i + 1 < parts:
            rem = rem - hi.astype(F32)
    return out


def _mm_parts(a_parts, b_parts, dims):
    dg = lambda a, b: lax.dot_general(a, b, dims, preferred_element_type=F32)
    acc = dg(a_parts[0], b_parts[0])
    if len(a_parts) > 1:
        acc = acc + dg(a_parts[0], b_parts[1]) + dg(a_parts[1], b_parts[0])
    return acc


def _mm(a, b, dims=NN, passes=1):
    n = 2 if passes == 3 else 1
    return _mm_parts(_split(a, n), _split(b, n), dims)


def _cumsum_rows(x):
    tri = _tril_mask(x.shape[0]).astype(BF16)
    hi, mid, lo = (jnp.dot(tri, part, preferred_element_type=F32) for part in _split(x, 3))
    return hi + (mid + lo)


def _softplus(z):
    return jnp.maximum(z, 0.0) + jnp.log1p(jnp.exp(-jnp.abs(z)))


def _log_sigmoid(z):
    return -_softplus(-z)


def _tril_mask(n, strict=False):
    r = lax.broadcasted_iota(jnp.int32, (n, n), 0)
    c = lax.broadcasted_iota(jnp.int32, (n, n), 1)
    return (r > c) if strict else (r >= c)


def _params(n_axes):
    return pltpu.CompilerParams(
        dimension_semantics=("arbitrary",) * n_axes, vmem_limit_bytes=VMEM_LIMIT)


def _norm_matmul_kernel(h_ref, g_ref, w_ref, o_ref, u_ref):
    @pl.when(pl.program_id(1) == 0)
    def _():
        x = h_ref[...]
        ms = jnp.mean(x * x, axis=-1, keepdims=True)
        u_ref[...] = (x * lax.rsqrt(ms + EPS) * g_ref[...]).astype(BF16)

    o_ref[...] = jnp.dot(u_ref[...], w_ref[...], preferred_element_type=F32)


def _norm_matmul(h, g, w, tm, tn):
    m, d = h.shape
    n = w.shape[1]
    return pl.pallas_call(
        _norm_matmul_kernel,
        grid=(m // tm, n // tn),
        in_specs=[
            pl.BlockSpec((tm, d), lambda i, j: (i, 0)),
            pl.BlockSpec((1, d), lambda i, j: (0, 0)),
            pl.BlockSpec((d, tn), lambda i, j: (0, j)),
        ],
        out_specs=[pl.BlockSpec((tm, tn), lambda i, j: (i, j)),
                   pl.BlockSpec((tm, d), lambda i, j: (i, 0))],
        out_shape=[jax.ShapeDtypeStruct((m, n), F32), jax.ShapeDtypeStruct((m, d), BF16)],
        compiler_params=_params(2),
    )(h, g, w)


def _matmul_kernel(u_ref, w_ref, o_ref):
    o_ref[...] = jnp.dot(u_ref[...], w_ref[...], preferred_element_type=F32)


def _matmul(u, w, tm, tn):
    m, d = u.shape
    n = w.shape[1]
    return pl.pallas_call(
        _matmul_kernel,
        grid=(m // tm, n // tn),
        in_specs=[pl.BlockSpec((tm, d), lambda i, j: (i, 0)),
                  pl.BlockSpec((d, tn), lambda i, j: (0, j))],
        out_specs=pl.BlockSpec((tm, tn), lambda i, j: (i, j)),
        out_shape=jax.ShapeDtypeStruct((m, n), F32),
        compiler_params=_params(2),
    )(u, w)


def _seqs_per_step(bsz, want):
    return want if bsz % want == 0 else 1


def _round_robin(gens):
    gens = list(gens)
    while gens:
        alive = []
        for gen in gens:
            try:
                next(gen)
                alive.append(gen)
            except StopIteration:
                pass
        gens = alive


def _rwkv_kernel(p_ref, *rest, width, d_rank, a_rank, g_rank):
    s_ref, prev_ref = rest[-2:]

    @pl.when(pl.program_id(1) == 0)
    def _():
        s_ref[...] = jnp.zeros_like(s_ref)
        prev_ref[...] = jnp.zeros_like(prev_ref)

    _round_robin(_rwkv_chunk(i, p_ref, *rest, width=width, d_rank=d_rank, a_rank=a_rank,
                             g_rank=g_rank) for i in range(p_ref.shape[0]))


def _rwkv_chunk(i, p_ref, mu_ref, w0_ref, w2_ref, a0_ref, a2_ref, g2_ref, kk_ref, ka_ref,
                rk_ref, lnw_ref, lnb_ref, o_ref, s_ref, prev_ref, *, width, d_rank,
                a_rank, g_rank):
    c = CHUNK
    n = A_HEAD_DIM
    heads = width // n

    p = p_ref[i]
    row = lax.broadcasted_iota(jnp.int32, p.shape, 0)
    shifted = jnp.where(row == 0, prev_ref[i], pltpu.roll(p, 1, axis=0))
    prev_ref[i] = p[c - 1:c, :]
    x = p + mu_ref[...] * (shifted - p)

    off_w = 3 * width
    off_a = off_w + _rup(d_rank, LANE)
    off_g = off_a + _rup(a_rank, LANE)
    r = x[:, 0:width]
    k = x[:, width:2 * width]
    v = x[:, 2 * width:3 * width]
    xw = x[:, off_w:off_w + d_rank]
    xa = x[:, off_a:off_a + a_rank]
    xg = x[:, off_g:off_g + g_rank]

    w_log = -_softplus(-(w0_ref[...] + _mm(jnp.tanh(xw), w2_ref[...], NN, GATE_PASSES))) - 0.5
    ld = -jnp.exp(w_log)
    a = jax.nn.sigmoid(a0_ref[...] + _mm(xa, a2_ref[...], NN, GATE_PASSES))
    g = _mm(jax.nn.sigmoid(xg), g2_ref[...], NN, GATE_PASSES)
    kk = k * kk_ref[...]
    k2 = k * (1.0 + (a - 1.0) * ka_ref[...])

    yield
    cum = _cumsum_rows(ld)
    e_pos = jnp.exp(cum)
    e_neg = jnp.exp(-cum)
    e_prev = jnp.exp(cum - ld)
    e_end = jnp.exp(cum[c - 1:c, :] - cum)

    assert c == n and 2 * n == LANE
    pairs = heads // 2
    ps = [slice(j * LANE, (j + 1) * LANE) for j in range(pairs)]
    js = range(pairs)
    first = lax.broadcasted_iota(jnp.int32, (1, LANE), 1) < n
    rows = lax.broadcasted_iota(jnp.int32, (c, 4 * c), 0)
    cols = lax.broadcasted_iota(jnp.int32, (c, 4 * c), 1) % c
    low4 = rows >= cols
    slow2 = (rows > cols)[:, 0:LANE]
    eye2 = (rows == cols)[:, 0:LANE].astype(F32)

    def bdiag(x):
        return jnp.concatenate([jnp.where(first, x, 0.0), jnp.where(first, 0.0, x)], axis=0)

    same_head = (lax.broadcasted_iota(jnp.int32, (LANE, LANE), 0) // n
                 == lax.broadcasted_iota(jnp.int32, (LANE, LANE), 1) // n).astype(BF16)

    def head_sum(x, parts):
        acc = None
        for part in reversed(_split(x, parts)):
            t = jnp.concatenate([jnp.dot(part[:, sl], same_head, preferred_element_type=F32)
                                 for sl in ps], axis=1)
            acc = t if acc is None else acc + t
        return acc

    yield
    kk = kk / jnp.maximum(jnp.sqrt(head_sum(kk * kk, 2)), 1e-12)
    b = kk * a
    lhs_f = jnp.concatenate([-kk * e_prev, r * e_pos], axis=0)
    rhs_f = jnp.concatenate([b * e_neg, k2 * e_neg], axis=0)
    end_f = jnp.concatenate([b * e_end, k2 * e_end], axis=0)

    yield
    s0 = [s_ref[i, j] for j in js]
    big = [_mm(lhs_f[:, ps[j]],
               jnp.concatenate([bdiag(rhs_f[0:c, ps[j]]), bdiag(rhs_f[c:2 * c, ps[j]]),
                                bdiag(s0[j])], axis=0), NT, A_PASSES) for j in js]

    yield
    a_ab = [jnp.where(slow2, big[j][0:c, 0:LANE], 0.0) for j in js]
    v_bd = [bdiag(v[:, ps[j]]) for j in js]
    rhs_u = [big[j][0:c, 2 * LANE:3 * LANE]
             + _mm(jnp.where(slow2, big[j][0:c, LANE:2 * LANE], 0.0), v_bd[j], NN, A_PASSES)
             for j in js]

    yield
    tinv = [eye2 + a_ab[j] for j in js]
    pw = [_mm(a_ab[j], bdiag(a_ab[j]), NN, A_INV_PASSES) for j in js]
    for _ in range((c - 1).bit_length() - 2):
        yield
        res = [_mm(jnp.concatenate([pw[j], tinv[j]], axis=0), bdiag(pw[j]), NN, A_INV_PASSES)
               for j in js]
        pw = [res[j][0:c] for j in js]
        tinv = [tinv[j] + res[j][c:2 * c] for j in js]
    yield
    tinv = [tinv[j] + _mm(tinv[j], bdiag(pw[j]), NN, A_INV_PASSES) for j in js]
    yield
    u = [_mm(tinv[j], bdiag(rhs_u[j]), NN, A_INV_PASSES) for j in js]
    yield
    y = [big[j][c:2 * c, 2 * LANE:3 * LANE]
         + _mm(jnp.where(low4, big[j][c:2 * c, 0:2 * LANE], 0.0),
               jnp.concatenate([bdiag(u[j]), v_bd[j]], axis=0), NN, A_PASSES) for j in js]
    for j in js:
        uv = jnp.concatenate([u[j], v[:, ps[j]]], axis=0)
        full = _mm(uv, end_f[:, ps[j]], TN, A_PASSES)
        s_ref[i, j] = (s0[j] * e_pos[c - 1:c, ps[j]]
                       + jnp.where(first, full[0:n], full[n:2 * n]))

    yield
    y = jnp.concatenate(y, axis=1)
    sums = head_sum(jnp.concatenate([y, r * k2 * rk_ref[...]], axis=0), 2)
    yc = y - sums[0:c] * (1.0 / n)
    var = head_sum(yc * yc, 2) * (1.0 / n)
    yn = yc * lax.rsqrt(var + A_GN_EPS) * lnw_ref[...] + lnb_ref[...]
    bonus = sums[c:2 * c] * v
    o_ref[i] = ((yn + bonus) * g).astype(o_ref.dtype)


def _rwkv_mixer(p, mu, w0, w2, a0, a2, g2, k_k, k_a, r_k, ln_w, ln_b, out_dtype=BF16):
    bsz, lp, cols = p.shape
    nb = _seqs_per_step(bsz, A_SEQS_PER_STEP)
    width = w0.shape[-1]
    heads = width // A_HEAD_DIM
    kern = functools.partial(_rwkv_kernel, width=width, d_rank=w2.shape[0],
                             a_rank=a2.shape[0], g_rank=g2.shape[0])
    full = lambda arr: pl.BlockSpec(arr.shape, lambda b, c: (0,) * arr.ndim)
    small = (mu, w0, w2, a0, a2, g2, k_k, k_a, r_k, ln_w, ln_b)
    return pl.pallas_call(
        kern,
        grid=(bsz // nb, lp // CHUNK),
        in_specs=[pl.BlockSpec((nb, CHUNK, cols), lambda b, c: (b, c, 0))]
        + [full(t) for t in small],
        out_specs=pl.BlockSpec((nb, CHUNK, width), lambda b, c: (b, c, 0)),
        out_shape=jax.ShapeDtypeStruct((bsz, lp, width), out_dtype),
        scratch_shapes=[pltpu.VMEM((nb, heads // 2, A_HEAD_DIM, 2 * A_HEAD_DIM), F32),
                        pltpu.VMEM((nb, 1, cols), F32)],
        compiler_params=_params(2),
    )(p, *small)


def _gla_kernel(p_ref, *rest, qk_width, v_width, rank):
    s_ref = rest[-1]

    @pl.when(pl.program_id(1) == 0)
    def _():
        s_ref[...] = jnp.zeros_like(s_ref)

    _round_robin(_gla_chunk(i, p_ref, *rest, qk_width=qk_width, v_width=v_width, rank=rank)
                 for i in range(p_ref.shape[0]))


def _gla_chunk(i, p_ref, a2_ref, ab_ref, nw_ref, o_ref, s_ref, *, qk_width, v_width, rank):
    c = CHUNK
    dv = B_DV
    heads = v_width // dv
    dk = qk_width // heads

    off_v = 2 * qk_width
    off_a = off_v + v_width
    off_g = off_a + _rup(rank, LANE)
    p = p_ref[i]
    q = p[:, 0:qk_width] * (dk ** -0.5)
    k = p[:, qk_width:off_v]
    v = p[:, off_v:off_a]
    xa = p[:, off_a:off_a + rank]
    gate = p[:, off_g:off_g + v_width]

    lg = _log_sigmoid(_mm(xa, a2_ref[...], NN, GATE_PASSES) + ab_ref[...]) / B_GATE_NORMALIZER
    yield
    bcum = _cumsum_rows(lg)
    low = _tril_mask(c)
    b_last = bcum[c - 1:c, :]
    q_in = q * jnp.exp(bcum)
    k_end = k * jnp.exp(b_last - bcum)
    e_last = jnp.exp(b_last)

    qf, kf = [], []
    for i0 in range(0, c, B_SUB):
        i1 = i0 + B_SUB
        ref_pt = bcum[i1 - 1:i1, :]
        qf.append(q[i0:i1] * jnp.exp(bcum[i0:i1] - ref_pt))
        kf.append(k[0:i1] * jnp.exp(ref_pt - bcum[0:i1]))

    hs = range(heads)
    sls = [slice(h * dk, (h + 1) * dk) for h in hs]
    vss = [slice(h * dv, (h + 1) * dv) for h in hs]
    yield
    st = [s_ref[i, h] for h in hs]
    o = [_mm(q_in[:, sls[h]], st[h], NT, B_PASSES) for h in hs]
    starts = list(range(0, c, B_SUB))
    sc = [[_mm(qf[bi][:, sls[h]], kf[bi][:, sls[h]], NT, B_PASSES)
           for bi in range(len(starts))] for h in hs]
    yield
    sc = [[jnp.where(low[i0:i0 + B_SUB, 0:i0 + B_SUB], sc[h][bi], 0.0)
           for bi, i0 in enumerate(starts)] for h in hs]
    for h in hs:
        blocks = [_mm(sc[h][bi], v[0:i0 + B_SUB, vss[h]], NN, B_PASSES)
                  for bi, i0 in enumerate(starts)]
        o[h] = o[h] + jnp.concatenate(blocks, axis=0)
    for h in hs:
        s_ref[i, h] = st[h] * e_last[:, sls[h]] + _mm(v[:, vss[h]], k_end[:, sls[h]], TN, B_PASSES)
    yield
    for h in hs:
        vs = vss[h]
        o_h = o[h] * lax.rsqrt(jnp.mean(o[h] * o[h], axis=-1, keepdims=True) + EPS) * nw_ref[...]
        g_h = gate[:, vs]
        o_ref[i, :, vs] = (o_h * (g_h * jax.nn.sigmoid(g_h))).astype(o_ref.dtype)


def _gla_mixer(p, a2, ab, norm_w, out_dtype=BF16):
    bsz, lp, cols = p.shape
    rank, qk_width = a2.shape
    dv = norm_w.shape[-1]
    heads = qk_width // (dv // 2)
    v_width = heads * dv
    nb = _seqs_per_step(bsz, B_SEQS_PER_STEP)
    kern = functools.partial(_gla_kernel, qk_width=qk_width, v_width=v_width, rank=rank)
    full = lambda arr: pl.BlockSpec(arr.shape, lambda b, c: (0,) * arr.ndim)
    small = (a2, ab, norm_w)
    return pl.pallas_call(
        kern,
        grid=(bsz // nb, lp // CHUNK),
        in_specs=[pl.BlockSpec((nb, CHUNK, cols), lambda b, c: (b, c, 0))]
        + [full(t) for t in small],
        out_specs=pl.BlockSpec((nb, CHUNK, v_width), lambda b, c: (b, c, 0)),
        out_shape=jax.ShapeDtypeStruct((bsz, lp, v_width), out_dtype),
        scratch_shapes=[pltpu.VMEM((nb, heads, dv, qk_width // heads), F32)],
        compiler_params=_params(2),
    )(p, *small)


def _mlstm_kernel(p_ref, *rest, qk_width, v_width, heads):
    @pl.when(pl.program_id(1) == 0)
    def _():
        for ref in rest[-4:]:
            ref[...] = jnp.zeros_like(ref)

    _round_robin(_mlstm_chunk(i, p_ref, *rest, qk_width=qk_width, v_width=v_width,
                              heads=heads) for i in range(p_ref.shape[0]))


def _mlstm_chunk(i, p_ref, cw_ref, cb_ref, gb_ref, nw_ref, o_ref, cm_ref, n_ref, m_ref,
                 hist_ref, *, qk_width, v_width, heads):
    c = CHUNK
    dqk = qk_width // heads
    dv = v_width // heads

    off_v = 2 * qk_width
    off_g = off_v + v_width
    off_o = off_g + LANE
    p = p_ref[i]
    qk_raw = p[:, 0:off_v]
    v = p[:, off_v:off_g]
    gates = p[:, off_g:off_o]
    og = p[:, off_o:off_o + v_width]

    ext = jnp.concatenate([hist_ref[i], qk_raw], axis=0)
    hist_ref[i] = qk_raw[c - 8:c, :]
    conv = cb_ref[...] + ext * cw_ref[C_CONV - 1:C_CONV, :]
    for d in range(1, C_CONV):
        conv = conv + pltpu.roll(ext, d, axis=0) * cw_ref[C_CONV - 1 - d:C_CONV - d, :]
    conv = conv[8:, :]
    qk = conv * jax.nn.sigmoid(conv)
    q = qk[:, 0:qk_width]
    k = qk[:, qk_width:off_v] * (dqk ** -0.5)

    capped = C_GATE_CAP * jnp.tanh((gates + gb_ref[...]) / C_GATE_CAP)
    lf = _log_sigmoid(capped)
    yield
    bcol = _cumsum_rows(lf)
    brow = bcol.T
    irow = capped.T
    low = _tril_mask(c)
    hs = range(heads)
    qss = [slice(h * dqk, (h + 1) * dqk) for h in hs]
    vss = [slice(h * dv, (h + 1) * dv) for h in hs]
    np_ = 2 if C_PASSES == 3 else 1

    q_p = [_split(q[:, qss[h]], np_) for h in hs]
    k_p = [_split(k[:, qss[h]], np_) for h in hs]
    yield
    cm = [cm_ref[i, h] for h in hs]
    qk_s = [_mm_parts(q_p[h], k_p[h], NT) for h in hs]
    q_cm = [_mm_parts(q_p[h], _split(cm[h], np_), NN) for h in hs]

    yield
    bc = [bcol[:, heads + h:heads + h + 1] for h in hs]
    ic = [capped[:, h:h + 1] for h in hs]
    m_prev = [m_ref[i, h:h + 1, 0:1] for h in hs]
    n_row = [n_ref[i, h:h + 1, :] for h in hs]
    log_w = [jnp.where(low, bc[h] - brow[heads + h:heads + h + 1, :] + irow[h:h + 1, :],
                       -jnp.inf) for h in hs]
    log_prev = [bc[h] + m_prev[h] for h in hs]
    m_t = [jnp.maximum(log_prev[h], jnp.max(log_w[h], axis=-1, keepdims=True)) for h in hs]
    w_prev = [jnp.exp(log_prev[h] - m_t[h]) for h in hs]
    s = [qk_s[h] * jnp.exp(log_w[h] - m_t[h]) for h in hs]
    yield
    num = [w_prev[h] * q_cm[h] + _mm(s[h], v[:, vss[h]], NN, C_PASSES) for h in hs]
    den = [w_prev[h] * jnp.sum(q[:, qss[h]] * n_row[h], axis=-1, keepdims=True)
           + jnp.sum(s[h], axis=-1, keepdims=True) for h in hs]
    hh = [num[h] / jnp.maximum(jnp.abs(den[h]), jnp.exp(-m_t[h])) for h in hs]

    yield
    m_new = [m_t[h][c - 1:c, :] for h in hs]
    b_last = [bc[h][c - 1:c, :] for h in hs]
    kw = [k[:, qss[h]] * jnp.exp(b_last[h] - bc[h] + ic[h] - m_new[h]) for h in hs]
    f_end = [jnp.exp(b_last[h] + m_prev[h] - m_new[h]) for h in hs]
    for h in hs:
        cm_ref[i, h] = f_end[h] * cm[h] + _mm(kw[h], v[:, vss[h]], TN, C_PASSES)
        n_ref[i, h:h + 1, :] = f_end[h] * n_row[h] + jnp.sum(kw[h], axis=0, keepdims=True)
        m_ref[i, h:h + 1, :] = jnp.broadcast_to(m_new[h], (1, m_ref.shape[-1]))

    yield
    for h in hs:
        vs = vss[h]
        hn = hh[h] * lax.rsqrt(jnp.mean(hh[h] * hh[h], axis=-1, keepdims=True) + EPS)
        o_ref[i, :, vs] = (hn * nw_ref[:, vs] * jax.nn.sigmoid(og[:, vs])).astype(o_ref.dtype)


def _mlstm_mixer(p, conv_w, conv_b, gate_b, norm_w, heads, out_dtype=BF16):
    bsz, lp, cols = p.shape
    qk2 = conv_w.shape[-1]
    qk_width = qk2 // 2
    v_width = norm_w.shape[-1]
    nb = _seqs_per_step(bsz, C_SEQS_PER_STEP)
    kern = functools.partial(_mlstm_kernel, qk_width=qk_width, v_width=v_width, heads=heads)
    full = lambda arr: pl.BlockSpec(arr.shape, lambda b, c: (0,) * arr.ndim)
    small = (conv_w, conv_b, gate_b, norm_w)
    return pl.pallas_call(
        kern,
        grid=(bsz // nb, lp // CHUNK),
        in_specs=[pl.BlockSpec((nb, CHUNK, cols), lambda b, c: (b, c, 0))]
        + [full(t) for t in small],
        out_specs=pl.BlockSpec((nb, CHUNK, v_width), lambda b, c: (b, c, 0)),
        out_shape=jax.ShapeDtypeStruct((bsz, lp, v_width), out_dtype),
        scratch_shapes=[pltpu.VMEM((nb, heads, qk_width // heads, v_width // heads), F32),
                        pltpu.VMEM((nb, 8, qk_width // heads), F32),
                        pltpu.VMEM((nb, 8, LANE), F32),
                        pltpu.VMEM((nb, 8, qk2), F32)],
        compiler_params=_params(2),
    )(p, *small)


def _out_proj_kernel(h_ref, ya_ref, yb_ref, yc_ref, wa_ref, wb_ref, wc_ref, o_ref):
    acc = jnp.dot(ya_ref[...], wa_ref[...], preferred_element_type=F32)
    acc += jnp.dot(yb_ref[...], wb_ref[...], preferred_element_type=F32)
    acc += jnp.dot(yc_ref[...], wc_ref[...], preferred_element_type=F32)
    o_ref[...] = h_ref[...] + acc


def _out_proj(h, ya, yb, yc, wa, wb, wc, tm):
    m, d = h.shape
    row = lambda arr: pl.BlockSpec((tm, arr.shape[1]), lambda i: (i, 0))
    full = lambda arr: pl.BlockSpec(arr.shape, lambda i: (0, 0))
    return pl.pallas_call(
        _out_proj_kernel,
        grid=(m // tm,),
        in_specs=[row(h), row(ya), row(yb), row(yc), full(wa), full(wb), full(wc)],
        out_specs=pl.BlockSpec((tm, d), lambda i: (i, 0)),
        out_shape=jax.ShapeDtypeStruct((m, d), F32),
        compiler_params=_params(1),
    )(h, ya, yb, yc, wa, wb, wc)


def _ffn_kernel(h_ref, g_ref, w1_ref, w3_ref, w2_ref, gf_ref, o_ref, u_ref, *, final_norm,
                f_axis):
    f = pl.program_id(f_axis)

    @pl.when(f == 0)
    def _():
        x = h_ref[...]
        ms = jnp.mean(x * x, axis=-1, keepdims=True)
        u_ref[...] = (x * lax.rsqrt(ms + EPS) * g_ref[...]).astype(BF16)
        o_ref[...] = x

    u = u_ref[...]
    a = jnp.dot(u, w1_ref[...], preferred_element_type=F32)
    b = jnp.dot(u, w3_ref[...], preferred_element_type=F32)
    act = (a * jax.nn.sigmoid(a) * b).astype(BF16)
    o_ref[...] += jnp.dot(act, w2_ref[...], preferred_element_type=F32)

    if final_norm:
        @pl.when(f == pl.num_programs(f_axis) - 1)
        def _():
            y = o_ref[...]
            ms = jnp.mean(y * y, axis=-1, keepdims=True)
            o_ref[...] = y * lax.rsqrt(ms + EPS) * gf_ref[...]


def _ffn(h, g, w1, w3, w2, g_final, tm, tf, final_norm):
    m, d = h.shape
    dff = w1.shape[1]
    return pl.pallas_call(
        functools.partial(_ffn_kernel, final_norm=final_norm, f_axis=1),
        grid=(m // tm, dff // tf),
        in_specs=[
            pl.BlockSpec((tm, d), lambda i, f: (i, 0)),
            pl.BlockSpec((1, d), lambda i, f: (0, 0)),
            pl.BlockSpec((d, tf), lambda i, f: (0, f)),
            pl.BlockSpec((d, tf), lambda i, f: (0, f)),
            pl.BlockSpec((tf, d), lambda i, f: (f, 0)),
            pl.BlockSpec((1, d), lambda i, f: (0, 0)),
        ],
        out_specs=pl.BlockSpec((tm, d), lambda i, f: (i, 0)),
        out_shape=jax.ShapeDtypeStruct((m, d), F32),
        scratch_shapes=[pltpu.VMEM((tm, d), BF16)],
        compiler_params=_params(2),
    )(h, g, w1, w3, w2, g_final)


def _ffn_last(h3, g, w1, w3, w2, g_final, row0, rows, tm, tf):
    bsz, lp, d = h3.shape
    dff = w1.shape[1]
    return pl.pallas_call(
        functools.partial(_ffn_kernel, final_norm=True, f_axis=2),
        grid=(bsz, rows // tm, dff // tf),
        in_specs=[
            pl.BlockSpec((pl.Element(tm), pl.Element(d)),
                         lambda b, i, f: (pl.multiple_of(b * lp + row0 + i * tm, 8), 0)),
            pl.BlockSpec((1, d), lambda b, i, f: (0, 0)),
            pl.BlockSpec((d, tf), lambda b, i, f: (0, f)),
            pl.BlockSpec((d, tf), lambda b, i, f: (0, f)),
            pl.BlockSpec((tf, d), lambda b, i, f: (f, 0)),
            pl.BlockSpec((1, d), lambda b, i, f: (0, 0)),
        ],
        out_specs=pl.BlockSpec((None, tm, d), lambda b, i, f: (b, i, 0)),
        out_shape=jax.ShapeDtypeStruct((bsz, rows, d), F32),
        scratch_shapes=[pltpu.VMEM((tm, d), BF16)],
        compiler_params=_params(3),
    )(h3.reshape(bsz * lp, d), g, w1, w3, w2, g_final)


def _pad_groups(w, sizes, axis=-1):
    pieces = []
    off = 0
    for s in sizes:
        piece = lax.slice_in_dim(w, off, off + s, axis=axis)
        pad = _rup(s, LANE) - s
        if pad:
            cfg = [(0, 0)] * w.ndim
            cfg[axis] = (0, pad)
            piece = jnp.pad(piece, cfg)
        pieces.append(piece)
        off += s
    return jnp.concatenate(pieces, axis=axis)


def _tile_rows(m):
    for t in (832, 640, 512, 320, 256, 128, 64, 32, 16, 8):
        if m % t == 0:
            return t
    return m


def _proj_tiles(m, d, n, in_bytes):
    budget = VMEM_LIMIT * 3 // 4
    col_tiles = [t for t in range(n, 0, -LANE) if n % t == 0]
    row_tiles = [t for t in (640, 320, 256, 128, 64, 32, 16, 8) if m % t == 0] or [m]
    for tn in col_tiles:
        for tm in row_tiles:
            need = 2 * tm * d * in_bytes + 2 * d * tn * 2 + 2 * tm * tn * 4
            if need <= budget:
                return tm, tn
    return row_tiles[-1], col_tiles[-1]


def _tile_cols(n, cap):
    best = LANE
    for t in range(LANE, cap + 1, LANE):
        if n % t == 0:
            best = t
    return best


def kernel(x, meta_tokens, norm_mix, w_in, rw_mu, rw_w0, rw_w2, rw_a0, rw_a2, rw_g2, rw_kk, rw_ka, rw_rk, rw_ln_w, rw_ln_b, gla_a2, gla_ab, gla_norm, ml_conv_w, ml_conv_b, ml_ib, ml_fb, ml_norm, w_out, norm_ffn, ffn_w1, ffn_w3, ffn_w2, norm_final):
    bsz, seq, d = x.shape
    depth = w_in.shape[0]
    n_meta = meta_tokens.shape[0]
    ltot = n_meta + seq
    lp = _rup(ltot, CHUNK)

    a_width = rw_w0.shape[-1]
    d_rank, a_rank, g_rank = rw_w2.shape[1], rw_a2.shape[1], rw_g2.shape[1]
    a_sizes = [a_width, a_width, a_width, d_rank, a_rank, g_rank]
    b_rank, b_qk = gla_a2.shape[1], gla_a2.shape[2]
    b_heads = b_qk // (gla_norm.shape[-1] // 2)
    b_width = b_heads * gla_norm.shape[-1]
    b_sizes = [b_qk, b_qk, b_width, b_rank, b_width]
    c_heads = ml_ib.shape[-1]
    c_qk = ml_conv_w.shape[-1] // 2
    c_width = ml_norm.shape[-1]
    c_sizes = [c_qk, c_qk, c_width, 2 * c_heads, c_width]
    a_cols, b_cols = sum(a_sizes), sum(b_sizes)

    meta = jnp.broadcast_to(meta_tokens[None].astype(x.dtype), (bsz, n_meta, d))
    h = jnp.concatenate([meta, x, jnp.zeros((bsz, lp - ltot, d), x.dtype)], axis=1)
    h = h.reshape(bsz * lp, d)
    m = bsz * lp
    tm = _tile_rows(m)
    row2 = lambda t: t.reshape(1, -1)

    w_out_bf = w_out.astype(BF16)
    w1_bf, w3_bf, w2_bf = ffn_w1.astype(BF16), ffn_w3.astype(BF16), ffn_w2.astype(BF16)
    for l in range(depth):
        w_l = w_in[l]
        wa = _pad_groups(w_l[:, :a_cols], a_sizes).astype(BF16)
        wb = _pad_groups(w_l[:, a_cols:a_cols + b_cols], b_sizes).astype(BF16)
        wc = _pad_groups(w_l[:, a_cols + b_cols:], c_sizes).astype(BF16)
        p_a, u = _norm_matmul(h, row2(norm_mix[l]), wa, *_proj_tiles(m, d, wa.shape[1], 6))
        p_b = _matmul(u, wb, *_proj_tiles(m, d, wb.shape[1], 2))
        p_c = _matmul(u, wc, *_proj_tiles(m, d, wc.shape[1], 2))

        y_a = _rwkv_mixer(
            p_a.reshape(bsz, lp, -1), _pad_groups(row2(rw_mu[l]), a_sizes),
            row2(rw_w0[l]), rw_w2[l], row2(rw_a0[l]), rw_a2[l], rw_g2[l], row2(rw_kk[l]),
            row2(rw_ka[l]), row2(rw_rk[l]), row2(rw_ln_w[l]), row2(rw_ln_b[l]))
        y_b = _gla_mixer(p_b.reshape(bsz, lp, -1), gla_a2[l], row2(gla_ab[l]),
                         row2(gla_norm[l]))
        gate_b = jnp.pad(jnp.concatenate([ml_ib[l], ml_fb[l]]), (0, LANE - 2 * c_heads))
        y_c = _mlstm_mixer(p_c.reshape(bsz, lp, -1), ml_conv_w[l], row2(ml_conv_b[l]),
                           row2(gate_b), row2(ml_norm[l]), c_heads)

        wo = w_out_bf[l]
        h = _out_proj(h, y_a.reshape(m, -1), y_b.reshape(m, -1), y_c.reshape(m, -1),
                      wo[:a_width], wo[a_width:a_width + b_width], wo[a_width + b_width:], tm)
        tf = _tile_cols(ffn_w1.shape[-1], 512)
        ffn_args = (row2(norm_ffn[l]), w1_bf[l], w3_bf[l], w2_bf[l], row2(norm_final))
        if l + 1 < depth:
            h = _ffn(h, *ffn_args, tm, tf, False)
        else:
            out = _ffn_last(h.reshape(bsz, lp, d), *ffn_args, n_meta, seq,
                            _tile_cols(seq, 1024), tf)
    return out
```
